```python
import jax, jax.numpy as jnp
from jax import lax
import numpy as np

D_MODEL = 1024
BATCH = 4
SEQ = 8192
DEPTH = 2

N_META = 16
HGRN_HEAD_V = 128
HGRN_WIDTH = D_MODEL // 2
HGRN_HEADS = HGRN_WIDTH // HGRN_HEAD_V
HGRN_EXPAND = 128
HGRN_KDIM = HGRN_HEADS * HGRN_EXPAND
CHUNK = 64
POOL_WINDOWS = (2, 4, 8, 16)
POOL_GROUPS = len(POOL_WINDOWS)
POOL_WIDTH = D_MODEL // 2
POOL_GROUP_DIM = POOL_WIDTH // POOL_GROUPS
D_FF = 2816
CONV_WIDTH = 3
EPS = 1e-6
LOG_FLOOR = 1e-30
SPLITS = (HGRN_KDIM, HGRN_KDIM, HGRN_WIDTH, HGRN_WIDTH, POOL_WIDTH, D_MODEL, D_MODEL)
IN_COLS = sum(SPLITS)

kernel_name = "hybrid_hgrn2_pool_convffn_trunk"


def rmsnorm(x, gain):
    xf = x.astype(jnp.float32)
    y = xf * lax.rsqrt(jnp.mean(xf * xf, axis=-1, keepdims=True) + EPS)
    return (y * gain.astype(jnp.float32)).astype(x.dtype)


def _hgrn2_chunk_step(state, xs):
    q, k, v, lf = xs
    c = q.shape[2]
    b = jnp.cumsum(lf, axis=2)
    o_inter = jnp.einsum('bhtk,bhkv->bhtv', q * jnp.exp(b), state)
    diff = b[:, :, :, None, :] - b[:, :, None, :, :]
    causal = (jnp.arange(c)[:, None] >= jnp.arange(c)[None, :])[None, None, :, :, None]
    decay = jnp.where(causal, jnp.exp(jnp.where(causal, diff, 0.0)), 0.0)
    scores = jnp.einsum('bhtk,bhtsk,bhsk->bhts', q, decay, k)
    o_intra = jnp.einsum('bhts,bhsv->bhtv', scores, v)
    b_last = b[:, :, -1, :]
    k_dec = k * jnp.exp(b_last[:, :, None, :] - b)
    new_state = jnp.exp(b_last)[..., None] * state + jnp.einsum('bhsk,bhsv->bhkv', k_dec, v)
    return new_state, o_inter + o_intra


def hgrn2(q, f_logit, i, g, lb, out_gain):
    bsz, L, _ = q.shape
    lb = lb.astype(jnp.float32)
    fl = f_logit.astype(jnp.float32)
    f = lb + (1.0 - lb) * jax.nn.sigmoid(fl)
    log_f = jnp.log(jnp.maximum(f, LOG_FLOOR))
    k = (1.0 - lb) * jax.nn.sigmoid(-fl)
    pad = CHUNK - N_META
    def prep(t, d):
        t = jnp.pad(t.astype(jnp.float32), ((0, 0), (pad, 0), (0, 0)))
        n = t.shape[1] // CHUNK
        return t.reshape(bsz, n, CHUNK, HGRN_HEADS, d).transpose(1, 0, 3, 2, 4)
    xs = (prep(q, HGRN_EXPAND), prep(k, HGRN_EXPAND), prep(i, HGRN_HEAD_V), prep(log_f, HGRN_EXPAND))
    state0 = jnp.zeros((bsz, HGRN_HEADS, HGRN_EXPAND, HGRN_HEAD_V), jnp.float32)
    _, o = lax.scan(_hgrn2_chunk_step, state0, xs)
    o = o.transpose(1, 0, 3, 2, 4).reshape(bsz, -1, HGRN_HEADS, HGRN_HEAD_V)[:, pad:]
    o = o * lax.rsqrt(jnp.mean(o * o, axis=-1, keepdims=True) + EPS)
    o = o.reshape(bsz, L, HGRN_WIDTH) * out_gain.astype(jnp.float32)
    return o * jax.nn.sigmoid(g.astype(jnp.float32))


def multiscale_pool(v, proj, scale):
    bsz, L, _ = v.shape
    vg = v.astype(jnp.float32).reshape(bsz, L, POOL_GROUPS, POOL_GROUP_DIM)
    cs0 = jnp.pad(jnp.cumsum(vg, axis=1), ((0, 0), (1, 0), (0, 0), (0, 0)))
    t1 = jnp.arange(1, L + 1)
    outs = []
    for gi, w in enumerate(POOL_WINDOWS):
        hi = cs0[:, 1:, gi]
        lo = jnp.pad(cs0[:, :L + 1 - w, gi], ((0, 0), (w - 1, 0), (0, 0)))
        cnt = jnp.minimum(t1, w).astype(jnp.float32)[None, :, None]
        outs.append((hi - lo) / cnt - vg[:, :, gi])
    pooled = jnp.stack(outs, axis=2)
    y = jnp.einsum('blgc,gcd->blgd', pooled, proj.astype(jnp.float32))
    return y.reshape(bsz, L, POOL_WIDTH) * scale.astype(jnp.float32)


def causal_dwconv(x, w, b):
    L = x.shape[1]
    xp = jnp.pad(x, ((0, 0), (CONV_WIDTH - 1, 0), (0, 0)))
    y = b
    for j in range(CONV_WIDTH):
        y = y + xp[:, j:j + L] * w[j]
    return y


def setup_inputs(seed: int = 0) -> dict:
    key = jax.random.key(seed)
    ks = jax.random.split(key, 24)
    nrm = lambda k, shape, s: jax.random.normal(k, shape, jnp.float32) * s
    gain = lambda k, shape: 1.0 + nrm(k, shape, 0.05)
    return {
        "x": nrm(ks[0], (BATCH, SEQ, D_MODEL), 1.0),
        "meta_tokens": nrm(ks[1], (N_META, D_MODEL), 1.0),
        "mix_norm_pre": gain(ks[2], (DEPTH, D_MODEL)),
        "mix_norm_post": gain(ks[3], (DEPTH, D_MODEL)),
        "w_in": nrm(ks[4], (DEPTH, D_MODEL, IN_COLS), D_MODEL ** -0.5),
        "hgrn_lower_bounds": nrm(ks[5], (DEPTH, HGRN_KDIM), 0.5),
        "hgrn_out_norm": gain(ks[6], (DEPTH, HGRN_WIDTH)),
        "w_branch_hgrn": nrm(ks[7], (DEPTH, HGRN_WIDTH, D_MODEL), HGRN_WIDTH ** -0.5),
        "pool_proj": nrm(ks[8], (DEPTH, POOL_GROUPS, POOL_GROUP_DIM, POOL_GROUP_DIM), POOL_GROUP_DIM ** -0.5),
        "pool_scale": gain(ks[9], (DEPTH, POOL_WIDTH)),
        "w_branch_pool": nrm(ks[10], (DEPTH, POOL_WIDTH, D_MODEL), POOL_WIDTH ** -0.5),
        "w_out": nrm(ks[11], (DEPTH, D_MODEL, D_MODEL), D_MODEL ** -0.5),
        "ffn_norm_pre": gain(ks[12], (DEPTH, D_MODEL)),
        "ffn_norm_post": gain(ks[13], (DEPTH, D_MODEL)),
        "ffn_w_gate": nrm(ks[14], (DEPTH, D_MODEL, D_FF), D_MODEL ** -0.5),
        "ffn_w_up": nrm(ks[15], (DEPTH, D_MODEL, D_FF), D_MODEL ** -0.5),
        "ffn_conv_w": nrm(ks[16], (DEPTH, CONV_WIDTH, D_FF), CONV_WIDTH ** -0.5),
        "ffn_conv_b": nrm(ks[17], (DEPTH, D_FF), 0.02),
        "ffn_w_down": nrm(ks[18], (DEPTH, D_FF, D_MODEL), D_FF ** -0.5),
    }


def reference(x, meta_tokens, mix_norm_pre, mix_norm_post, w_in, hgrn_lower_bounds, hgrn_out_norm,
              w_branch_hgrn, pool_proj, pool_scale, w_branch_pool, w_out, ffn_norm_pre, ffn_norm_post,
              ffn_w_gate, ffn_w_up, ffn_conv_w, ffn_conv_b, ffn_w_down):
    bsz = x.shape[0]
    meta = jnp.broadcast_to(meta_tokens[None].astype(x.dtype), (bsz, N_META, D_MODEL))
    h = jnp.concatenate([meta, x], axis=1)
    gam = jax.nn.softmax(hgrn_lower_bounds.astype(jnp.float32), axis=0)
    lbs = jnp.clip(jnp.cumsum(gam, axis=0) - gam[0], 0.0, 1.0)
    cuts = list(np.cumsum(SPLITS)[:-1])
    for l in range(DEPTH):
        u = rmsnorm(h, mix_norm_pre[l])
        proj = u @ w_in[l]
        q, f_logit, i_in, g_out, v_pool, gate_a, gate_b = jnp.split(proj, cuts, axis=-1)
        a = hgrn2(q, f_logit, i_in, g_out, lbs[l], hgrn_out_norm[l])
        p = multiscale_pool(v_pool, pool_proj[l], pool_scale[l])
        z = (jax.nn.sigmoid(gate_a.astype(jnp.float32)) * (a @ w_branch_hgrn[l])
             + jax.nn.sigmoid(gate_b.astype(jnp.float32)) * (p @ w_branch_pool[l]))
        h = h + rmsnorm(z @ w_out[l], mix_norm_post[l])
        u = rmsnorm(h, ffn_norm_pre[l])
        gt = causal_dwconv(u @ ffn_w_gate[l], ffn_conv_w[l], ffn_conv_b[l])
        y = (jax.nn.gelu(gt, approximate=True) * (u @ ffn_w_up[l])) @ ffn_w_down[l]
        h = h + rmsnorm(y, ffn_norm_post[l])
    return h[:, N_META:]
```

```python
import functools

import numpy as np
import jax
import jax.numpy as jnp
from jax import lax
from jax.experimental import pallas as pl
from jax.experimental.pallas import tpu as pltpu

D_MODEL = 1024
DEPTH = 2
N_META = 16
HEADS = 4
HEAD_K = 128
HEAD_V = 128
KDIM = HEADS * HEAD_K
WIDTH = HEADS * HEAD_V
CHUNK = 64
POOL_WINDOWS = (2, 4, 8, 16)
POOL_GROUP = 128
POOL_WIDTH = POOL_GROUP * len(POOL_WINDOWS)
POOL_HIST = 16
D_FF = 2816
FF_BLOCK = 256
CONV_WIDTH = 3
CONV_HIST = 8
EPS = 1e-6
LOG_FLOOR = 1e-30
IN_COLS = 2 * KDIM + 2 * WIDTH + POOL_WIDTH + 2 * D_MODEL

_Q0, _F0, _I0, _G0, _P0, _A0, _B0 = (0, KDIM, 2 * KDIM, 2 * KDIM + WIDTH, 2 * KDIM + 2 * WIDTH,
                                     2 * KDIM + 2 * WIDTH + POOL_WIDTH,
                                     2 * KDIM + 2 * WIDTH + POOL_WIDTH + D_MODEL)

LEVELS = (32, 16, 8, 4, 2, 1)
VMEM_LIMIT_BYTES = 60 * 1024 * 1024

F32 = jnp.float32
BF16 = jnp.bfloat16


def _decay_sum_matrix():
    c64 = CHUNK
    w = np.zeros(((2 + len(LEVELS)) * c64, c64), np.float32)
    for t in range(c64):
        w[t, :t + 1] = 1.0
        w[c64 + t, t + 1:] = 1.0
    for li, c in enumerate(LEVELS):
        base = (2 + li) * c64
        for r in range(c64):
            mid = r - r % (2 * c) + c - 1
            if r > mid:
                w[base + r, mid + 1:r + 1] = 1.0
            else:
                w[base + r, r + 1:mid + 1] = 1.0
    return w


def _rms(x, gain):
    ms = jnp.mean(x * x, axis=-1, keepdims=True)
    return x * lax.rsqrt(ms + EPS) * gain


def _sigmoid(x):
    return 1.0 / (1.0 + jnp.exp(-x))


def _dot(a, b):
    return jnp.dot(a, b, preferred_element_type=F32)


def _dot_nt(a, b):
    return lax.dot_general(a, b, (((1,), (1,)), ((), ())), preferred_element_type=F32)


def _lower_bound(lbraw, layer):
    rows = [lbraw[j:j + 1, :] for j in range(DEPTH)]
    m = functools.reduce(jnp.maximum, rows)
    es = [jnp.exp(r - m) for r in rows]
    tot = functools.reduce(lambda a, b: a + b, es)
    gam = [e / tot for e in es]
    cum = functools.reduce(lambda a, b: a + b, gam[:layer + 1])
    return jnp.clip(cum - gam[0], 0.0, 1.0)


def _mixer_kernel(h_ref, st0_ref, ph0_ref, gpre_ref, win_ref, lbraw_ref, ogain_ref, wbh_ref, pproj_ref,
                  pscale_ref, wbp_ref, wout_ref, gpost_ref, wexp_ref, *rest, tile, layer, meta):
    if meta:
        o_ref, st_out_ref, ph_out_ref, proj_scr, state_scr, pbuf_scr, a_scr = rest
    else:
        o_ref, proj_scr, state_scr, pbuf_scr, a_scr = rest
    T = tile
    C = CHUNK

    @pl.when(pl.program_id(1) == 0)
    def _():
        state_scr[...] = st0_ref[...]
        pbuf_scr[0:POOL_HIST, :] = ph0_ref[...]

    h = h_ref[...]
    u = _rms(h, gpre_ref[...]).astype(BF16)
    proj_scr[...] = _dot(u, win_ref[...])

    lb = _lower_bound(lbraw_ref[...], layer)
    oml = 1.0 - lb
    ogain = ogain_ref[...]
    wexp = wexp_ref[...]
    ti = lax.broadcasted_iota(jnp.int32, (C, C), 0)
    si = lax.broadcasted_iota(jnp.int32, (C, C), 1)
    xor = ti ^ si
    level_masks = [(xor >= c) & (xor < 2 * c) for c in LEVELS]
    causal = ti >= si

    def chunk_body(ci, carry):
        rows = pl.ds(pl.multiple_of(ci * C, C), C)
        fl = proj_scr[rows, _F0:_F0 + KDIM]
        e = jnp.exp(-jnp.abs(fl))
        r = 1.0 / (1.0 + e)
        er = e * r
        nonneg = fl >= 0.0
        sig = jnp.where(nonneg, r, er)
        sig_neg = jnp.where(nonneg, er, r)
        f = lb + oml * sig
        lf = jnp.log(jnp.maximum(f, LOG_FLOOR))
        kk = oml * sig_neg
        lf_hi = lf.astype(BF16)
        lf_lo = (lf - lf_hi.astype(F32)).astype(BF16)
        decay = jnp.exp(_dot(wexp, lf_hi) + _dot(wexp, lf_lo))
        q = proj_scr[rows, _Q0:_Q0 + KDIM]
        v = proj_scr[rows, _I0:_I0 + WIDTH]
        g = proj_scr[rows, _G0:_G0 + WIDTH]
        for hd in range(HEADS):
            sl = slice(hd * HEAD_K, (hd + 1) * HEAD_K)
            qh, kh = q[:, sl], kk[:, sl]
            vb = v[:, sl].astype(BF16)
            sc = _dot_nt(qh.astype(BF16), kh.astype(BF16))
            for li in range(len(LEVELS)):
                dl = decay[(2 + li) * C:(3 + li) * C, sl]
                s_l = _dot_nt((qh * dl).astype(BF16), (kh * dl).astype(BF16))
                sc = jnp.where(level_masks[li], s_l, sc)
            sc = jnp.where(causal, sc, 0.0)
            st_t = state_scr[hd]
            o = _dot(sc.astype(BF16), vb) + _dot_nt((qh * decay[0:C, sl]).astype(BF16), st_t.astype(BF16))
            kdec = (kh * decay[C:2 * C, sl]).astype(BF16)
            state_scr[hd] = st_t * decay[C - 1:C, sl] + _dot(v[:, sl].T.astype(BF16), kdec)
            o = o * lax.rsqrt(jnp.mean(o * o, axis=-1, keepdims=True) + EPS)
            a_scr[rows, sl] = o * ogain[:, sl] * _sigmoid(g[:, sl])
        return carry

    lax.fori_loop(0, T // C, chunk_body, 0)

    vp = proj_scr[:, _P0:_P0 + POOL_WIDTH]
    pbuf_scr[POOL_HIST:POOL_HIST + T, :] = vp
    ys = []
    for gi, w in enumerate(POOL_WINDOWS):
        cs = slice(gi * POOL_GROUP, (gi + 1) * POOL_GROUP)
        s = vp[:, cs]
        for j in range(1, w):
            s = s + pbuf_scr[POOL_HIST - j:POOL_HIST - j + T, cs]
        if meta:
            pos = lax.broadcasted_iota(jnp.int32, (T, 1), 0) - (T - N_META)
            cnt = jnp.clip(pos + 1, 1, w).astype(F32)
            pooled = s / cnt - vp[:, cs]
        else:
            pooled = s * (1.0 / w) - vp[:, cs]
        ys.append(_dot(pooled.astype(BF16), pproj_ref[gi]))
    p = jnp.concatenate(ys, axis=-1) * pscale_ref[...]
    hist = pbuf_scr[T:T + POOL_HIST, :]
    pbuf_scr[0:POOL_HIST, :] = hist

    br_a = _dot(a_scr[...].astype(BF16), wbh_ref[...])
    br_p = _dot(p.astype(BF16), wbp_ref[...])
    z = (_sigmoid(proj_scr[:, _A0:_A0 + D_MODEL]) * br_a
         + _sigmoid(proj_scr[:, _B0:_B0 + D_MODEL]) * br_p)
    zz = _dot(z.astype(BF16), wout_ref[...])
    o_ref[...] = h + _rms(zz, gpost_ref[...])

    if meta:
        st_out_ref[...] = state_scr[...]
        ph_out_ref[...] = hist


def _ffn_kernel(h_ref, c0_ref, gpre_ref, wg_ref, wu_ref, cw_ref, cb_ref, wd_ref, gpost_ref, *rest,
                tile, meta):
    if meta:
        o_ref, c_out_ref, gbuf_scr, ubuf_scr, act_scr = rest
    else:
        o_ref, gbuf_scr, ubuf_scr, act_scr = rest
    T = tile
    H = CONV_HIST
    nblk = D_FF // FF_BLOCK

    @pl.when(pl.program_id(1) == 0)
    def _():
        gbuf_scr[0:H, :] = c0_ref[...]

    h = h_ref[...]
    u = _rms(h, gpre_ref[...]).astype(BF16)
    gbuf_scr[H:H + T, :] = _dot(u, wg_ref[...])
    ubuf_scr[...] = _dot(u, wu_ref[...])

    def col_body(j, carry):
        cols = pl.ds(pl.multiple_of(j * FF_BLOCK, FF_BLOCK), FF_BLOCK)
        cw = cw_ref[:, cols]
        gc = cb_ref[:, cols]
        for k in range(CONV_WIDTH):
            off = H - (CONV_WIDTH - 1) + k
            gc = gc + gbuf_scr[off:off + T, cols] * cw[k:k + 1, :]
        inner = 0.7978845608028654 * (gc + 0.044715 * (gc * gc * gc))
        act = 0.5 * gc * (1.0 + jnp.tanh(inner)) * ubuf_scr[:, cols]
        act_scr[:, cols] = act.astype(BF16)
        return carry

    lax.fori_loop(0, nblk, col_body, 0)
    y = _dot(act_scr[...], wd_ref[...])
    o_ref[...] = h + _rms(y, gpost_ref[...])
    hist = gbuf_scr[T:T + H, :]
    gbuf_scr[0:H, :] = hist
    if meta:
        c_out_ref[...] = hist


def _const_spec(shape):
    nd = len(shape)
    return pl.BlockSpec(shape, lambda b, t, _nd=nd: (0,) * _nd, pipeline_mode=pl.Buffered(1))


def _mixer(h, st0, ph0, weights, wexp, *, tile, layer, meta):
    bsz, seq, _ = h.shape
    nt = seq // tile
    tok_spec = pl.BlockSpec((None, tile, D_MODEL), lambda b, t: (b, t, 0))
    consts = (st0, ph0) + tuple(weights) + (wexp,)
    out_shape = [jax.ShapeDtypeStruct(h.shape, F32)]
    out_specs = [tok_spec]
    if meta:
        out_shape += [jax.ShapeDtypeStruct(st0.shape, F32), jax.ShapeDtypeStruct(ph0.shape, F32)]
        out_specs += [_const_spec(st0.shape), _const_spec(ph0.shape)]
    out = pl.pallas_call(
        functools.partial(_mixer_kernel, tile=tile, layer=layer, meta=meta),
        grid=(bsz, nt),
        in_specs=[tok_spec] + [_const_spec(c.shape) for c in consts],
        out_specs=out_specs,
        out_shape=out_shape,
        scratch_shapes=[
            pltpu.VMEM((tile, IN_COLS), F32),
            pltpu.VMEM((HEADS, HEAD_V, HEAD_K), F32),
            pltpu.VMEM((POOL_HIST + tile, POOL_WIDTH), F32),
            pltpu.VMEM((tile, WIDTH), F32),
        ],
        compiler_params=pltpu.CompilerParams(
            dimension_semantics=("arbitrary", "arbitrary"), vmem_limit_bytes=VMEM_LIMIT_BYTES),
        name="mixer_meta" if meta else "mixer",
    )(h, *consts)
    return out if meta else out[0]


def _ffn(h, c0, weights, *, tile, meta):
    bsz, seq, _ = h.shape
    nt = seq // tile
    tok_spec = pl.BlockSpec((None, tile, D_MODEL), lambda b, t: (b, t, 0))
    consts = (c0,) + tuple(weights)
    out_shape = [jax.ShapeDtypeStruct(h.shape, F32)]
    out_specs = [tok_spec]
    if meta:
        out_shape += [jax.ShapeDtypeStruct(c0.shape, F32)]
        out_specs += [_const_spec(c0.shape)]
    out = pl.pallas_call(
        functools.partial(_ffn_kernel, tile=tile, meta=meta),
        grid=(bsz, nt),
        in_specs=[tok_spec] + [_const_spec(c.shape) for c in consts],
        out_specs=out_specs,
        out_shape=out_shape,
        scratch_shapes=[
            pltpu.VMEM((CONV_HIST + tile, D_FF), F32),
            pltpu.VMEM((tile, D_FF), F32),
            pltpu.VMEM((tile, D_FF), BF16),
        ],
        compiler_params=pltpu.CompilerParams(
            dimension_semantics=("arbitrary", "arbitrary"), vmem_limit_bytes=VMEM_LIMIT_BYTES),
        name="ffn_meta" if meta else "ffn",
    )(h, *consts)
    return out if meta else out[0]


def _pick_tile(seq):
    for t in (512, 256, 128, 64):
        if seq % t == 0:
            return t
    raise ValueError(f"sequence length {seq} must be a multiple of {CHUNK}")


def kernel(x, meta_tokens, mix_norm_pre, mix_norm_post, w_in, hgrn_lower_bounds, hgrn_out_norm, w_branch_hgrn, pool_proj, pool_scale, w_branch_pool, w_out, ffn_norm_pre, ffn_norm_post, ffn_w_gate, ffn_w_up, ffn_conv_w, ffn_conv_b, ffn_w_down):
    bsz, seq, d = x.shape
    assert d == D_MODEL and meta_tokens.shape == (N_META, D_MODEL)
    tile = _pick_tile(seq)
    wexp = jnp.asarray(_decay_sum_matrix(), BF16)
    row = lambda a: a.reshape(1, -1).astype(F32)

    h = x.astype(F32)
    hm = jnp.concatenate([jnp.zeros((CHUNK - N_META, D_MODEL), F32), meta_tokens.astype(F32)], axis=0)[None]
    st_zero = jnp.zeros((HEADS, HEAD_V, HEAD_K), F32)
    ph_zero = jnp.zeros((POOL_HIST, POOL_WIDTH), F32)
    cv_zero = jnp.zeros((CONV_HIST, D_FF), F32)
    lbraw = hgrn_lower_bounds.astype(F32)

    for l in range(DEPTH):
        mix_w = (row(mix_norm_pre[l]), w_in[l].astype(BF16), lbraw, row(hgrn_out_norm[l]),
                 w_branch_hgrn[l].astype(BF16), pool_proj[l].astype(BF16), row(pool_scale[l]),
                 w_branch_pool[l].astype(BF16), w_out[l].astype(BF16), row(mix_norm_post[l]))
        ffn_w = (row(ffn_norm_pre[l]), ffn_w_gate[l].astype(BF16), ffn_w_up[l].astype(BF16),
                 ffn_conv_w[l].astype(F32), row(ffn_conv_b[l]), ffn_w_down[l].astype(BF16),
                 row(ffn_norm_post[l]))
        hm, st, ph = _mixer(hm, st_zero, ph_zero, mix_w, wexp, tile=CHUNK, layer=l, meta=True)
        h = _mixer(h, st, ph, mix_w, wexp, tile=tile, layer=l, meta=False)
        hm, cv = _ffn(hm, cv_zero, ffn_w, tile=CHUNK, meta=True)
        h = _ffn(h, cv, ffn_w, tile=tile, meta=False)
    return h.astype(x.dtype)
```

```python
import functools
import math

import numpy as np
import jax
import jax.numpy as jnp
from jax import lax
from jax.experimental import pallas as pl
from jax.experimental.pallas import tpu as pltpu

D_MODEL = 1024
DEPTH = 2
N_META = 16
HEADS = 4
HEAD_K = 128
HEAD_V = 128
KDIM = HEADS * HEAD_K
WIDTH = HEADS * HEAD_V
CHUNK = 64
POOL_WINDOWS = (2, 4, 8, 16)
POOL_GROUP = 128
POOL_WIDTH = POOL_GROUP * len(POOL_WINDOWS)
POOL_HIST = 16
D_FF = 2816
FF_BLOCK = 256
CONV_WIDTH = 3
SUBLANES = 8
EPS = 1e-6
LOG_FLOOR = 1e-30
IN_COLS = 2 * KDIM + 2 * WIDTH + POOL_WIDTH + 2 * D_MODEL

_Q0, _F0, _I0, _G0, _P0, _A0, _B0 = (0, KDIM, 2 * KDIM, 2 * KDIM + WIDTH, 2 * KDIM + 2 * WIDTH,
                                     2 * KDIM + 2 * WIDTH + POOL_WIDTH,
                                     2 * KDIM + 2 * WIDTH + POOL_WIDTH + D_MODEL)

LEVELS = (32, 16, 8, 4, 2, 1)
VMEM_LIMIT_BYTES = 60 * 1024 * 1024

_GELU_K1 = -2.0 * math.sqrt(2.0 / math.pi) * math.log2(math.e)
_GELU_K3 = _GELU_K1 * 0.044715

F32 = jnp.float32
BF16 = jnp.bfloat16


def _decay_sum_matrix():
    c64 = CHUNK
    w = np.zeros(((2 + len(LEVELS)) * c64, c64), np.float32)
    for t in range(c64):
        w[t, :t + 1] = 1.0
        w[c64 + t, t + 1:] = 1.0
    for li, c in enumerate(LEVELS):
        base = (2 + li) * c64
        for r in range(c64):
            mid = r - r % (2 * c) + c - 1
            if r > mid:
                w[base + r, mid + 1:r + 1] = 1.0
            else:
                w[base + r, r + 1:mid + 1] = 1.0
    return w


def _rms(x, gain):
    ms = jnp.mean(x * x, axis=-1, keepdims=True)
    return x * lax.rsqrt(ms + EPS) * gain


def _sigmoid(x):
    return 1.0 / (1.0 + jnp.exp(-x))


def _dot(a, b):
    return jnp.dot(a, b, preferred_element_type=F32)


def _dot_nt(a, b):
    return lax.dot_general(a, b, (((1,), (1,)), ((), ())), preferred_element_type=F32)


def _lower_bound(lbraw, layer):
    rows = [lbraw[j:j + 1, :] for j in range(DEPTH)]
    m = functools.reduce(jnp.maximum, rows)
    es = [jnp.exp(r - m) for r in rows]
    tot = functools.reduce(lambda a, b: a + b, es)
    gam = [e / tot for e in es]
    cum = functools.reduce(lambda a, b: a + b, gam[:layer + 1])
    return jnp.clip(cum - gam[0], 0.0, 1.0)


def _mixer_kernel(h_ref, st0_ref, ph0_ref, gpre_ref, win_ref, lbraw_ref, ogain_ref, wbh_ref, pproj_ref,
                  pscale_ref, wbp_ref, wout_ref, gpost_ref, wexp_ref, *rest, tile, layer, meta):
    if meta:
        o_ref, st_out_ref, ph_out_ref, *scratch = rest
    else:
        o_ref, *scratch = rest
    proj_scr, state_scr, pbuf_scr, a_scr, o_scr, qb_scr, inc_scr, sprev_scr, dlast_scr = scratch
    T = tile
    C = CHUNK

    @pl.when(pl.program_id(1) == 0)
    def _():
        state_scr[...] = st0_ref[...]
        pbuf_scr[0:POOL_HIST, :] = ph0_ref[...]

    h = h_ref[...]
    u = _rms(h, gpre_ref[...]).astype(BF16)
    proj_scr[...] = _dot(u, win_ref[...])

    lb = _lower_bound(lbraw_ref[...], layer)
    oml = 1.0 - lb
    ogain = ogain_ref[...]
    wexp = wexp_ref[...]
    ti = lax.broadcasted_iota(jnp.int32, (C, C), 0)
    si = lax.broadcasted_iota(jnp.int32, (C, C), 1)
    xor = ti ^ si
    level_masks = [(xor >= c) & (xor < 2 * c) for c in LEVELS]
    causal = ti >= si

    def intra_chunk(ci):
        rows = pl.ds(pl.multiple_of(ci * C, C), C)
        fl = proj_scr[rows, _F0:_F0 + KDIM]
        e = jnp.exp(-jnp.abs(fl))
        r = 1.0 / (1.0 + e)
        er = e * r
        nonneg = fl >= 0.0
        sig = jnp.where(nonneg, r, er)
        sig_neg = jnp.where(nonneg, er, r)
        f = lb + oml * sig
        lf = jnp.log(jnp.maximum(f, LOG_FLOOR))
        kk = oml * sig_neg
        lf_hi = lf.astype(BF16)
        lf_lo = (lf - lf_hi.astype(F32)).astype(BF16)
        decay = jnp.exp(_dot(wexp, lf_hi) + _dot(wexp, lf_lo))
        q = proj_scr[rows, _Q0:_Q0 + KDIM]
        v = proj_scr[rows, _I0:_I0 + WIDTH]
        dlast_scr[ci] = decay[C - 1:C, :]
        qb_scr[rows, :] = (q * decay[0:C, :]).astype(BF16)
        for hd in range(HEADS):
            sl = slice(hd * HEAD_K, (hd + 1) * HEAD_K)
            qh, kh = q[:, sl], kk[:, sl]
            sc = _dot_nt(qh.astype(BF16), kh.astype(BF16))
            for li in range(len(LEVELS)):
                dl = decay[(2 + li) * C:(3 + li) * C, sl]
                s_l = _dot_nt((qh * dl).astype(BF16), (kh * dl).astype(BF16))
                sc = jnp.where(level_masks[li], s_l, sc)
            sc = jnp.where(causal, sc, 0.0)
            o_scr[rows, sl] = _dot(sc.astype(BF16), v[:, sl].astype(BF16))
            kdec = (kh * decay[C:2 * C, sl]).astype(BF16)
            inc_scr[ci, hd] = _dot(v[:, sl].T.astype(BF16), kdec)

    def finish_chunk(ci):
        rows = pl.ds(pl.multiple_of(ci * C, C), C)
        g = proj_scr[rows, _G0:_G0 + WIDTH]
        for hd in range(HEADS):
            sl = slice(hd * HEAD_K, (hd + 1) * HEAD_K)
            o = o_scr[rows, sl] + _dot_nt(qb_scr[rows, sl], sprev_scr[ci, hd])
            o = o * lax.rsqrt(jnp.mean(o * o, axis=-1, keepdims=True) + EPS)
            a_scr[rows, sl] = o * ogain[:, sl] * _sigmoid(g[:, sl])

    nch = T // C
    unroll = 2 if nch % 2 == 0 else 1

    def run_chunks(fn):
        def body(i, carry):
            for k in range(unroll):
                fn(i * unroll + k)
            return carry
        lax.fori_loop(0, nch // unroll, body, 0)

    run_chunks(intra_chunk)
    for hd in range(HEADS):
        sl = slice(hd * HEAD_K, (hd + 1) * HEAD_K)
        st_t = state_scr[hd]
        for ci in range(nch):
            sprev_scr[ci, hd] = st_t.astype(BF16)
            st_t = st_t * dlast_scr[ci][:, sl] + inc_scr[ci, hd]
        state_scr[hd] = st_t
    run_chunks(finish_chunk)

    vp = proj_scr[:, _P0:_P0 + POOL_WIDTH]
    pbuf_scr[POOL_HIST:POOL_HIST + T, :] = vp
    ys = []
    for gi, w in enumerate(POOL_WINDOWS):
        cs = slice(gi * POOL_GROUP, (gi + 1) * POOL_GROUP)
        s = vp[:, cs]
        for j in range(1, w):
            s = s + pbuf_scr[POOL_HIST - j:POOL_HIST - j + T, cs]
        if meta:
            pos = lax.broadcasted_iota(jnp.int32, (T, 1), 0) - (T - N_META)
            cnt = jnp.clip(pos + 1, 1, w).astype(F32)
            pooled = s / cnt - vp[:, cs]
        else:
            pooled = s * (1.0 / w) - vp[:, cs]
        ys.append(_dot(pooled.astype(BF16), pproj_ref[gi]))
    p = jnp.concatenate(ys, axis=-1) * pscale_ref[...]
    hist = pbuf_scr[T:T + POOL_HIST, :]
    pbuf_scr[0:POOL_HIST, :] = hist

    br_a = _dot(a_scr[...].astype(BF16), wbh_ref[...])
    br_p = _dot(p.astype(BF16), wbp_ref[...])
    z = (_sigmoid(proj_scr[:, _A0:_A0 + D_MODEL]) * br_a
         + _sigmoid(proj_scr[:, _B0:_B0 + D_MODEL]) * br_p)
    zz = _dot(z.astype(BF16), wout_ref[...])
    o_ref[...] = h + _rms(zz, gpost_ref[...])

    if meta:
        st_out_ref[...] = state_scr[...]
        ph_out_ref[...] = hist


def _ffn_kernel(h_ref, c0_ref, gpre_ref, wg_ref, wu_ref, cw_ref, cb_ref, wd_ref, gpost_ref, *rest,
                tile, meta):
    if meta:
        o_ref, c_out_ref, hist_scr, act_scr = rest
    else:
        o_ref, hist_scr, act_scr = rest
    T = tile

    @pl.when(pl.program_id(1) == 0)
    def _():
        hist_scr[...] = c0_ref[...]

    h = h_ref[...]
    u = _rms(h, gpre_ref[...]).astype(BF16)
    row8 = lax.broadcasted_iota(jnp.int32, (SUBLANES, FF_BLOCK), 0)

    def shifted(g, prev, k):
        rolled = pltpu.roll(g, k, 0)
        top = jnp.where(row8 < k, pltpu.roll(prev, k, 0), rolled[0:SUBLANES])
        return jnp.concatenate([top, rolled[SUBLANES:]], axis=0)

    for j in range(D_FF // FF_BLOCK):
        cols = slice(j * FF_BLOCK, (j + 1) * FF_BLOCK)
        g = _dot(u, wg_ref[:, cols])
        up = _dot(u, wu_ref[:, cols])
        prev = hist_scr[:, cols]
        hist_scr[:, cols] = g[T - SUBLANES:T]
        cw = cw_ref[:, cols]
        gc = cb_ref[:, cols] + shifted(g, prev, 2) * cw[0:1] + shifted(g, prev, 1) * cw[1:2] + g * cw[2:3]
        e = jnp.exp2(gc * (_GELU_K1 + _GELU_K3 * (gc * gc)))
        act_scr[:, cols] = ((gc * up) * (1.0 / (1.0 + e))).astype(BF16)

    y = _dot(act_scr[...], wd_ref[...])
    o_ref[...] = h + _rms(y, gpost_ref[...])
    if meta:
        c_out_ref[...] = hist_scr[...]


def _const_spec(shape):
    nd = len(shape)
    return pl.BlockSpec(shape, lambda b, t, _nd=nd: (0,) * _nd, pipeline_mode=pl.Buffered(1))


def _mixer(h, st0, ph0, weights, wexp, *, tile, layer, meta):
    bsz, seq, _ = h.shape
    nt = seq // tile
    nch = tile // CHUNK
    tok_spec = pl.BlockSpec((None, tile, D_MODEL), lambda b, t: (b, t, 0))
    consts = (st0, ph0) + tuple(weights) + (wexp,)
    out_shape = [jax.ShapeDtypeStruct(h.shape, F32)]
    out_specs = [tok_spec]
    if meta:
        out_shape += [jax.ShapeDtypeStruct(st0.shape, F32), jax.ShapeDtypeStruct(ph0.shape, F32)]
        out_specs += [_const_spec(st0.shape), _const_spec(ph0.shape)]
    out = pl.pallas_call(
        functools.partial(_mixer_kernel, tile=tile, layer=layer, meta=meta),
        grid=(bsz, nt),
        in_specs=[tok_spec] + [_const_spec(c.shape) for c in consts],
        out_specs=out_specs,
        out_shape=out_shape,
        scratch_shapes=[
            pltpu.VMEM((tile, IN_COLS), F32),
            pltpu.VMEM((HEADS, HEAD_V, HEAD_K), F32),
            pltpu.VMEM((POOL_HIST + tile, POOL_WIDTH), F32),
            pltpu.VMEM((tile, WIDTH), F32),
            pltpu.VMEM((tile, WIDTH), F32),
            pltpu.VMEM((tile, KDIM), BF16),
            pltpu.VMEM((nch, HEADS, HEAD_V, HEAD_K), F32),
            pltpu.VMEM((nch, HEADS, HEAD_V, HEAD_K), BF16),
            pltpu.VMEM((nch, 1, KDIM), F32),
        ],
        compiler_params=pltpu.CompilerParams(
            dimension_semantics=("arbitrary", "arbitrary"), vmem_limit_bytes=VMEM_LIMIT_BYTES),
        name="mixer_meta" if meta else "mixer",
    )(h, *consts)
    return out if meta else out[0]


def _ffn(h, c0, weights, *, tile, meta):
    bsz, seq, _ = h.shape
    nt = seq // tile
    tok_spec = pl.BlockSpec((None, tile, D_MODEL), lambda b, t: (b, t, 0))
    consts = (c0,) + tuple(weights)
    out_shape = [jax.ShapeDtypeStruct(h.shape, F32)]
    out_specs = [tok_spec]
    if meta:
        out_shape += [jax.ShapeDtypeStruct(c0.shape, F32)]
        out_specs += [_const_spec(c0.shape)]
    out = pl.pallas_call(
        functools.partial(_ffn_kernel, tile=tile, meta=meta),
        grid=(bsz, nt),
        in_specs=[tok_spec] + [_const_spec(c.shape) for c in consts],
        out_specs=out_specs,
        out_shape=out_shape,
        scratch_shapes=[
            pltpu.VMEM((SUBLANES, D_FF), F32),
            pltpu.VMEM((tile, D_FF), BF16),
        ],
        compiler_params=pltpu.CompilerParams(
            dimension_semantics=("arbitrary", "arbitrary"), vmem_limit_bytes=VMEM_LIMIT_BYTES),
        name="ffn_meta" if meta else "ffn",
    )(h, *consts)
    return out if meta else out[0]


def _pick_tile(seq):
    for t in (512, 256, 128, 64):
        if seq % t == 0:
            return t
    raise ValueError(f"sequence length {seq} must be a multiple of {CHUNK}")


def kernel(x, meta_tokens, mix_norm_pre, mix_norm_post, w_in, hgrn_lower_bounds, hgrn_out_norm, w_branch_hgrn, pool_proj, pool_scale, w_branch_pool, w_out, ffn_norm_pre, ffn_norm_post, ffn_w_gate, ffn_w_up, ffn_conv_w, ffn_conv_b, ffn_w_down):
    bsz, seq, d = x.shape
    assert d == D_MODEL and meta_tokens.shape == (N_META, D_MODEL)
    tile = _pick_tile(seq)
    wexp = jnp.asarray(_decay_sum_matrix(), BF16)
    row = lambda a: a.reshape(1, -1).astype(F32)

    h = x.astype(F32)
    hm = jnp.concatenate([jnp.zeros((CHUNK - N_META, D_MODEL), F32), meta_tokens.astype(F32)], axis=0)[None]
    st_zero = jnp.zeros((HEADS, HEAD_V, HEAD_K), F32)
    ph_zero = jnp.zeros((POOL_HIST, POOL_WIDTH), F32)
    cv_zero = jnp.zeros((SUBLANES, D_FF), F32)
    lbraw = hgrn_lower_bounds.astype(F32)

    for l in range(DEPTH):
        mix_w = (row(mix_norm_pre[l]), w_in[l].astype(BF16), lbraw, row(hgrn_out_norm[l]),
                 w_branch_hgrn[l].astype(BF16), pool_proj[l].astype(BF16), row(pool_scale[l]),
                 w_branch_pool[l].astype(BF16), w_out[l].astype(BF16), row(mix_norm_post[l]))
        ffn_w = (row(ffn_norm_pre[l]), ffn_w_gate[l].astype(BF16), ffn_w_up[l].astype(BF16),
                 ffn_conv_w[l].astype(F32), row(ffn_conv_b[l]), ffn_w_down[l].astype(BF16),
                 row(ffn_norm_post[l]))
        hm, st, ph = _mixer(hm, st_zero, ph_zero, mix_w, wexp, tile=CHUNK, layer=l, meta=True)
        h = _mixer(h, st, ph, mix_w, wexp, tile=tile, layer=l, meta=False)
        hm, cv = _ffn(hm, cv_zero, ffn_w, tile=CHUNK, meta=True)
        h = _ffn(h, cv, ffn_w, tile=tile, meta=False)
    return h.astype(x.dtype)
```

```python
import functools
import math

import numpy as np
import jax
import jax.numpy as jnp
from jax import lax
from jax.experimental import pallas as pl
from jax.experimental.pallas import tpu as pltpu

D_MODEL = 1024
DEPTH = 2
N_META = 16
HEADS = 4
HEAD_K = 128
HEAD_V = 128
KDIM = HEADS * HEAD_K
WIDTH = HEADS * HEAD_V
CHUNK = 64
POOL_WINDOWS = (2, 4, 8, 16)
POOL_GROUP = 128
POOL_WIDTH = POOL_GROUP * len(POOL_WINDOWS)
POOL_HIST = 16
D_FF = 2816
FF_BLOCK = 256
PROJ_BLOCK = 256
CONV_WIDTH = 3
SUBLANES = 8
EPS = 1e-6
LOG_FLOOR = 1e-30
IN_COLS = 2 * KDIM + 2 * WIDTH + POOL_WIDTH + 2 * D_MODEL

_Q0, _F0, _I0, _G0, _P0, _A0, _B0 = (0, KDIM, 2 * KDIM, 2 * KDIM + WIDTH, 2 * KDIM + 2 * WIDTH,
                                     2 * KDIM + 2 * WIDTH + POOL_WIDTH,
                                     2 * KDIM + 2 * WIDTH + POOL_WIDTH + D_MODEL)

LEVELS = (32, 16, 8, 4, 2, 1)
VMEM_LIMIT_BYTES = 60 * 1024 * 1024

_GELU_K1 = -2.0 * math.sqrt(2.0 / math.pi) * math.log2(math.e)
_GELU_K3 = _GELU_K1 * 0.044715

F32 = jnp.float32
BF16 = jnp.bfloat16


def _decay_sum_matrix():
    c64 = CHUNK
    w = np.zeros(((2 + len(LEVELS)) * c64, c64), np.float32)
    for t in range(c64):
        w[t, :t + 1] = 1.0
        w[c64 + t, t + 1:] = 1.0
    for li, c in enumerate(LEVELS):
        base = (2 + li) * c64
        for r in range(c64):
            mid = r - r % (2 * c) + c - 1
            if r > mid:
                w[base + r, mid + 1:r + 1] = 1.0
            else:
                w[base + r, r + 1:mid + 1] = 1.0
    return np.concatenate([w, w], axis=1)


def _rms(x, gain):
    ms = jnp.mean(x * x, axis=-1, keepdims=True)
    return x * lax.rsqrt(ms + EPS) * gain


def _sigmoid(x):
    return 1.0 / (1.0 + jnp.exp(-x))


def _dot(a, b):
    return jnp.dot(a, b, preferred_element_type=F32)


def _dot_nt(a, b):
    return lax.dot_general(a, b, (((1,), (1,)), ((), ())), preferred_element_type=F32)


def _lower_bound(lbraw, layer):
    rows = [lbraw[j:j + 1, :] for j in range(DEPTH)]
    m = functools.reduce(jnp.maximum, rows)
    es = [jnp.exp(r - m) for r in rows]
    tot = functools.reduce(lambda a, b: a + b, es)
    gam = [e / tot for e in es]
    cum = functools.reduce(lambda a, b: a + b, gam[:layer + 1])
    return jnp.clip(cum - gam[0], 0.0, 1.0)


def _mixer_kernel(h_ref, st0_ref, ph0_ref, gpre_ref, win_ref, lbraw_ref, ogain_ref, wbh_ref, pproj_ref,
                  pscale_ref, wbp_ref, wout_ref, gpost_ref, wexp_ref, *rest, tile, layer, meta):
    if meta:
        o_ref, st_out_ref, ph_out_ref, *scratch = rest
    else:
        o_ref, *scratch = rest
    proj_scr, late_scr, u_scr, state_scr, pbuf_scr, a_scr, o_scr, qb_scr, kk_scr, dec_scr, sprev_scr = scratch
    T = tile
    C = CHUNK

    @pl.when(pl.program_id(1) == 0)
    def _():
        state_scr[...] = st0_ref[...]
        pbuf_scr[0:POOL_HIST, :] = ph0_ref[...]

    h = h_ref[...]
    u_scr[...] = _rms(h, gpre_ref[...]).astype(BF16)
    nch = T // C
    proj_scr[...] = _dot(u_scr[...], win_ref[:, 0:_P0])
    late_blocks = (IN_COLS - _P0) // PROJ_BLOCK
    per_step = -(-late_blocks // (nch + 2))
    assert (nch + 1) * per_step <= late_blocks or nch < 4

    def project_late(i):
        first = (i + 1) * per_step
        count = per_step if not isinstance(i, int) else max(0, min(per_step, late_blocks - first))
        for b in range(count):
            start = (first + b) * PROJ_BLOCK
            if not isinstance(start, int):
                start = pl.multiple_of(start, PROJ_BLOCK)
            late_scr[:, pl.ds(start, PROJ_BLOCK)] = _dot(u_scr[...], win_ref[:, pl.ds(_P0 + start, PROJ_BLOCK)])

    lb = _lower_bound(lbraw_ref[...], layer)
    oml = 1.0 - lb
    ogain = ogain_ref[...]
    wexp = wexp_ref[...]
    ti = lax.broadcasted_iota(jnp.int32, (C, C), 0)
    si = lax.broadcasted_iota(jnp.int32, (C, C), 1)
    xor = ti ^ si
    level_masks = [(xor >= c) & (xor < 2 * c) for c in LEVELS]
    causal = ti >= si

    def chunk_rows(ci):
        return pl.ds(ci * C if isinstance(ci, int) else pl.multiple_of(ci * C, C), C)

    heads = [slice(hd * HEAD_K, (hd + 1) * HEAD_K) for hd in range(HEADS)]

    def step(i, par, finish=True, scores=True, gates=True):
        prv = 1 - par
        if finish:
            o_inter = [_dot_nt(qb_scr[prv, :, sl], sprev_scr[prv, hd]) for hd, sl in enumerate(heads)]
        project_late(i)
        if scores:
            rows = chunk_rows(i)
            dec = dec_scr.at[par]
            q = proj_scr[rows, _Q0:_Q0 + KDIM]
            v = proj_scr[rows, _I0:_I0 + WIDTH]
            kk = kk_scr[par]
            for hd, sl in enumerate(heads):
                kdec = (kk[:, sl] * dec[C:2 * C, sl]).astype(BF16)
                st_t = state_scr[hd]
                sprev_scr[par, hd] = st_t.astype(BF16)
                state_scr[hd] = st_t * dec[C - 1:C, sl] + _dot(v[:, sl].T.astype(BF16), kdec)
            qb_scr[par] = (q * dec[0:C, :]).astype(BF16)
            scs = []
            for hd, sl in enumerate(heads):
                qh, kh = q[:, sl], kk[:, sl]
                sc = _dot_nt(qh.astype(BF16), kh.astype(BF16))
                for li in range(len(LEVELS)):
                    dl = dec[(2 + li) * C:(3 + li) * C, sl]
                    s_l = _dot_nt((qh * dl).astype(BF16), (kh * dl).astype(BF16))
                    sc = jnp.where(level_masks[li], s_l, sc)
                scs.append(jnp.where(causal, sc, 0.0).astype(BF16))
        if gates:
            fl = proj_scr[chunk_rows(i + 1), _F0:_F0 + KDIM]
            e = jnp.exp(-jnp.abs(fl))
            r = 1.0 / (1.0 + e)
            er = e * r
            nonneg = fl >= 0.0
            sig = jnp.where(nonneg, r, er)
            sig_neg = jnp.where(nonneg, er, r)
            f = lb + oml * sig
            lf = jnp.log2(jnp.maximum(f, LOG_FLOOR))
            kk_scr[prv] = oml * sig_neg
            lf_hi = lf.astype(BF16)
            lf_lo = (lf - lf_hi.astype(F32)).astype(BF16)
            expo = _dot(wexp, jnp.concatenate([lf_hi, lf_lo], axis=0))
            dec_scr[prv] = jnp.exp2(expo)
        if finish:
            rows_f = chunk_rows(i - 1)
            g = proj_scr[rows_f, _G0:_G0 + WIDTH]
            for hd, sl in enumerate(heads):
                o = o_scr[prv, :, sl] + o_inter[hd]
                o = o * lax.rsqrt(jnp.mean(o * o, axis=-1, keepdims=True) + EPS)
                a_scr[rows_f, sl] = o * ogain[:, sl] * _sigmoid(g[:, sl])
        if scores:
            for hd, sl in enumerate(heads):
                o_scr[par, :, sl] = _dot(scs[hd], v[:, sl].astype(BF16))

    step(-1, 1, finish=False, scores=False)
    if nch == 1:
        step(0, 0, finish=False, gates=False)
    else:
        assert nch % 2 == 0
        step(0, 0, finish=False)

        def trip(k, carry):
            step(2 * k + 1, 1)
            step(2 * k + 2, 0)
            return carry
        lax.fori_loop(0, (nch - 2) // 2, trip, 0)
        step(nch - 1, 1, gates=False)
    step(nch, nch % 2, scores=False, gates=False)

    vp = late_scr[:, 0:POOL_WIDTH]
    pbuf_scr[POOL_HIST:POOL_HIST + T, :] = vp
    ys = []
    for gi, w in enumerate(POOL_WINDOWS):
        cs = slice(gi * POOL_GROUP, (gi + 1) * POOL_GROUP)
        s = vp[:, cs]
        for j in range(1, w):
            s = s + pbuf_scr[POOL_HIST - j:POOL_HIST - j + T, cs]
        if meta:
            pos = lax.broadcasted_iota(jnp.int32, (T, 1), 0) - (T - N_META)
            cnt = jnp.clip(pos + 1, 1, w).astype(F32)
            pooled = s / cnt - vp[:, cs]
        else:
            pooled = s * (1.0 / w) - vp[:, cs]
        ys.append(_dot(pooled.astype(BF16), pproj_ref[gi]))
    p = jnp.concatenate(ys, axis=-1) * pscale_ref[...]
    hist = pbuf_scr[T:T + POOL_HIST, :]
    pbuf_scr[0:POOL_HIST, :] = hist

    br_a = _dot(a_scr[...].astype(BF16), wbh_ref[...])
    br_p = _dot(p.astype(BF16), wbp_ref[...])
    z = (_sigmoid(late_scr[:, _A0 - _P0:_B0 - _P0]) * br_a
         + _sigmoid(late_scr[:, _B0 - _P0:IN_COLS - _P0]) * br_p)
    zz = _dot(z.astype(BF16), wout_ref[...])
    o_ref[...] = h + _rms(zz, gpost_ref[...])

    if meta:
        st_out_ref[...] = state_scr[...]
        ph_out_ref[...] = hist


def _ffn_kernel(h_ref, c0_ref, gpre_ref, wg_ref, wu_ref, cw_ref, cb_ref, wd_ref, gpost_ref, *rest,
                tile, meta):
    if meta:
        o_ref, c_out_ref, hist_scr, act_scr = rest
    else:
        o_ref, hist_scr, act_scr = rest
    T = tile

    @pl.when(pl.program_id(1) == 0)
    def _():
        hist_scr[...] = c0_ref[...]

    h = h_ref[...]
    u = _rms(h, gpre_ref[...]).astype(BF16)
    row8 = lax.broadcasted_iota(jnp.int32, (SUBLANES, FF_BLOCK), 0)

    def shifted(g, prev, k):
        rolled = pltpu.roll(g, k, 0)
        top = jnp.where(row8 < k, pltpu.roll(prev, k, 0), rolled[0:SUBLANES])
        return jnp.concatenate([top, rolled[SUBLANES:]], axis=0)

    for j in range(D_FF // FF_BLOCK):
        cols = slice(j * FF_BLOCK, (j + 1) * FF_BLOCK)
        g = _dot(u, wg_ref[:, cols])
        up = _dot(u, wu_ref[:, cols])
        prev = hist_scr[:, cols]
        hist_scr[:, cols] = g[T - SUBLANES:T]
        cw = cw_ref[:, cols]
        gc = cb_ref[:, cols] + shifted(g, prev, 2) * cw[0:1] + shifted(g, prev, 1) * cw[1:2] + g * cw[2:3]
        e = jnp.exp2(gc * (_GELU_K1 + _GELU_K3 * (gc * gc)))
        act_scr[:, cols] = ((gc * up) * (1.0 / (1.0 + e))).astype(BF16)

    y = _dot(act_scr[...], wd_ref[...])
    o_ref[...] = h + _rms(y, gpost_ref[...])
    if meta:
        c_out_ref[...] = hist_scr[...]


def _const_spec(shape):
    nd = len(shape)
    return pl.BlockSpec(shape, lambda b, t, _nd=nd: (0,) * _nd, pipeline_mode=pl.Buffered(1))


def _mixer(h, st0, ph0, weights, wexp, *, tile, layer, meta):
    bsz, seq, _ = h.shape
    nt = seq // tile
    tok_spec = pl.BlockSpec((None, tile, D_MODEL), lambda b, t: (b, t, 0))
    consts = (st0, ph0) + tuple(weights) + (wexp,)
    out_shape = [jax.ShapeDtypeStruct(h.shape, F32)]
    out_specs = [tok_spec]
    if meta:
        out_shape += [jax.ShapeDtypeStruct(st0.shape, F32), jax.ShapeDtypeStruct(ph0.shape, F32)]
        out_specs += [_const_spec(st0.shape), _const_spec(ph0.shape)]
    out = pl.pallas_call(
        functools.partial(_mixer_kernel, tile=tile, layer=layer, meta=meta),
        grid=(bsz, nt),
        in_specs=[tok_spec] + [_const_spec(c.shape) for c in consts],
        out_specs=out_specs,
        out_shape=out_shape,
        scratch_shapes=[
            pltpu.VMEM((tile, _P0), F32),
            pltpu.VMEM((tile, IN_COLS - _P0), F32),
            pltpu.VMEM((tile, D_MODEL), BF16),
            pltpu.VMEM((HEADS, HEAD_V, HEAD_K), F32),
            pltpu.VMEM((POOL_HIST + tile, POOL_WIDTH), F32),
            pltpu.VMEM((tile, WIDTH), F32),
            pltpu.VMEM((2, CHUNK, WIDTH), F32),
            pltpu.VMEM((2, CHUNK, KDIM), BF16),
            pltpu.VMEM((2, CHUNK, KDIM), F32),
            pltpu.VMEM((2, (2 + len(LEVELS)) * CHUNK, KDIM), F32),
            pltpu.VMEM((2, HEADS, HEAD_V, HEAD_K), BF16),
        ],
        compiler_params=pltpu.CompilerParams(
            dimension_semantics=("arbitrary", "arbitrary"), vmem_limit_bytes=VMEM_LIMIT_BYTES),
        name="mixer_meta" if meta else "mixer",
    )(h, *consts)
    return out if meta else out[0]


def _ffn(h, c0, weights, *, tile, meta):
    bsz, seq, _ = h.shape
    nt = seq // tile
    tok_spec = pl.BlockSpec((None, tile, D_MODEL), lambda b, t: (b, t, 0))
    consts = (c0,) + tuple(weights)
    out_shape = [jax.ShapeDtypeStruct(h.shape, F32)]
    out_specs = [tok_spec]
    if meta:
        out_shape += [jax.ShapeDtypeStruct(c0.shape, F32)]
        out_specs += [_const_spec(c0.shape)]
    out = pl.pallas_call(
        functools.partial(_ffn_kernel, tile=tile, meta=meta),
        grid=(bsz, nt),
        in_specs=[tok_spec] + [_const_spec(c.shape) for c in consts],
        out_specs=out_specs,
        out_shape=out_shape,
        scratch_shapes=[
            pltpu.VMEM((SUBLANES, D_FF), F32),
            pltpu.VMEM((tile, D_FF), BF16),
        ],
        compiler_params=pltpu.CompilerParams(
            dimension_semantics=("arbitrary", "arbitrary"), vmem_limit_bytes=VMEM_LIMIT_BYTES),
        name="ffn_meta" if meta else "ffn",
    )(h, *consts)
    return out if meta else out[0]


def _pick_tile(seq):
    for t in (512, 256, 128, 64):
        if seq % t == 0:
            return t
    raise ValueError(f"sequence length {seq} must be a multiple of {CHUNK}")


def kernel(x, meta_tokens, mix_norm_pre, mix_norm_post, w_in, hgrn_lower_bounds, hgrn_out_norm, w_branch_hgrn, pool_proj, pool_scale, w_branch_pool, w_out, ffn_norm_pre, ffn_norm_post, ffn_w_gate, ffn_w_up, ffn_conv_w, ffn_conv_b, ffn_w_down):
    bsz, seq, d = x.shape
    assert d == D_MODEL and meta_tokens.shape == (N_META, D_MODEL)
    tile = _pick_tile(seq)
    wexp = jnp.asarray(_decay_sum_matrix(), BF16)
    row = lambda a: a.reshape(1, -1).astype(F32)

    h = x.astype(F32)
    hm = jnp.concatenate([jnp.zeros((CHUNK - N_META, D_MODEL), F32), meta_tokens.astype(F32)], axis=0)[None]
    st_zero = jnp.zeros((HEADS, HEAD_V, HEAD_K), F32)
    ph_zero = jnp.zeros((POOL_HIST, POOL_WIDTH), F32)
    cv_zero = jnp.zeros((SUBLANES, D_FF), F32)
    lbraw = hgrn_lower_bounds.astype(F32)

    for l in range(DEPTH):
        mix_w = (row(mix_norm_pre[l]), w_in[l].astype(BF16), lbraw, row(hgrn_out_norm[l]),
                 w_branch_hgrn[l].astype(BF16), pool_proj[l].astype(BF16), row(pool_scale[l]),
                 w_branch_pool[l].astype(BF16), w_out[l].astype(BF16), row(mix_norm_post[l]))
        ffn_w = (row(ffn_norm_pre[l]), ffn_w_gate[l].astype(BF16), ffn_w_up[l].astype(BF16),
                 ffn_conv_w[l].astype(F32), row(ffn_conv_b[l]), ffn_w_down[l].astype(BF16),
                 row(ffn_norm_post[l]))
        hm, st, ph = _mixer(hm, st_zero, ph_zero, mix_w, wexp, tile=CHUNK, layer=l, meta=True)
        h = _mixer(h, st, ph, mix_w, wexp, tile=tile, layer=l, meta=False)
        hm, cv = _ffn(hm, cv_zero, ffn_w, tile=CHUNK, meta=True)
        h = _ffn(h, cv, ffn_w, tile=tile, meta=False)
    return h.astype(x.dtype)
```

```python
import functools
import math

import numpy as np
import jax
import jax.numpy as jnp
from jax import lax
from jax.experimental import pallas as pl
from jax.experimental.pallas import tpu as pltpu

D_MODEL = 1024
DEPTH = 2
N_META = 16
HEADS = 4
HEAD_K = 128
HEAD_V = 128
KDIM = HEADS * HEAD_K
WIDTH = HEADS * HEAD_V
CHUNK = 64
POOL_WINDOWS = (2, 4, 8, 16)
POOL_GROUP = 128
POOL_WIDTH = POOL_GROUP * len(POOL_WINDOWS)
POOL_HIST = 16
D_FF = 2816
FF_BLOCK = 256
PROJ_BLOCK = 256
CONV_WIDTH = 3
SUBLANES = 8
EPS = 1e-6
LOG_FLOOR = 1e-30
IN_COLS = 2 * KDIM + 2 * WIDTH + POOL_WIDTH + 2 * D_MODEL

_Q0, _F0, _I0, _G0, _P0, _A0, _B0 = (0, KDIM, 2 * KDIM, 2 * KDIM + WIDTH, 2 * KDIM + 2 * WIDTH,
                                     2 * KDIM + 2 * WIDTH + POOL_WIDTH,
                                     2 * KDIM + 2 * WIDTH + POOL_WIDTH + D_MODEL)

LEVELS = (32, 16, 8, 4, 2, 1)
VMEM_LIMIT_BYTES = 60 * 1024 * 1024

_GELU_K1 = -2.0 * math.sqrt(2.0 / math.pi) * math.log2(math.e)
_GELU_K3 = _GELU_K1 * 0.044715

F32 = jnp.float32
BF16 = jnp.bfloat16


def _decay_sum_matrix():
    c64 = CHUNK
    w = np.zeros(((2 + len(LEVELS)) * c64, c64), np.float32)
    for t in range(c64):
        w[t, :t + 1] = 1.0
        w[c64 + t, t + 1:] = 1.0
    for li, c in enumerate(LEVELS):
        base = (2 + li) * c64
        for r in range(c64):
            mid = r - r % (2 * c) + c - 1
            if r > mid:
                w[base + r, mid + 1:r + 1] = 1.0
            else:
                w[base + r, r + 1:mid + 1] = 1.0
    return np.concatenate([w, w], axis=1)


def _rms(x, gain):
    ms = jnp.mean(x * x, axis=-1, keepdims=True)
    return x * lax.rsqrt(ms + EPS) * gain


def _sigmoid(x):
    return 1.0 / (1.0 + jnp.exp(-x))


def _dot(a, b):
    return jnp.dot(a, b, preferred_element_type=F32)


def _dot_nt(a, b):
    return lax.dot_general(a, b, (((1,), (1,)), ((), ())), preferred_element_type=F32)


def _lower_bound(lbraw, layer):
    rows = [lbraw[j:j + 1, :] for j in range(DEPTH)]
    m = functools.reduce(jnp.maximum, rows)
    es = [jnp.exp(r - m) for r in rows]
    tot = functools.reduce(lambda a, b: a + b, es)
    gam = [e / tot for e in es]
    cum = functools.reduce(lambda a, b: a + b, gam[:layer + 1])
    return jnp.clip(cum - gam[0], 0.0, 1.0)


def _mixer_kernel(h_ref, st0_ref, ph0_ref, gpre_ref, win_ref, lbraw_ref, ogain_ref, wbh_ref, pproj_ref,
                  pscale_ref, wbp_ref, wout_ref, gpost_ref, wexp_ref, *rest, tile, layer, meta):
    if meta:
        o_ref, st_out_ref, ph_out_ref, *scratch = rest
    else:
        o_ref, *scratch = rest
    proj_scr, late_scr, u_scr, state_scr, pbuf_scr, a_scr, o_scr, qb_scr, kk_scr, dec_scr, sprev_scr = scratch
    T = tile
    C = CHUNK

    @pl.when(pl.program_id(1) == 0)
    def _():
        state_scr[...] = st0_ref[...]
        pbuf_scr[0:POOL_HIST, :] = ph0_ref[...]

    h = h_ref[...]
    u_scr[...] = _rms(h, gpre_ref[...]).astype(BF16)
    nch = T // C
    proj_scr[...] = _dot(u_scr[...], win_ref[:, 0:_P0])
    late_blocks = (IN_COLS - _P0) // PROJ_BLOCK
    per_step = -(-late_blocks // (nch + 2))
    assert (nch + 1) * per_step <= late_blocks or nch < 4

    def project_late(i):
        first = (i + 1) * per_step
        count = per_step if not isinstance(i, int) else max(0, min(per_step, late_blocks - first))
        for b in range(count):
            start = (first + b) * PROJ_BLOCK
            if not isinstance(start, int):
                start = pl.multiple_of(start, PROJ_BLOCK)
            late_scr[:, pl.ds(start, PROJ_BLOCK)] = _dot(u_scr[...], win_ref[:, pl.ds(_P0 + start, PROJ_BLOCK)])

    lb = _lower_bound(lbraw_ref[...], layer)
    oml = 1.0 - lb
    ogain = ogain_ref[...]
    wexp = wexp_ref[...]
    ti = lax.broadcasted_iota(jnp.int32, (C, C), 0)
    si = lax.broadcasted_iota(jnp.int32, (C, C), 1)
    xor = ti ^ si
    level_masks = [(xor >= c) & (xor < 2 * c) for c in LEVELS]
    causal = ti >= si

    def chunk_rows(ci):
        return pl.ds(ci * C if isinstance(ci, int) else pl.multiple_of(ci * C, C), C)

    heads = [slice(hd * HEAD_K, (hd + 1) * HEAD_K) for hd in range(HEADS)]

    def step(i, par, finish=True, scores=True, gates=True):
        prv = 1 - par
        if finish:
            o_inter = [_dot_nt(qb_scr[prv, :, sl], sprev_scr[prv, hd]) for hd, sl in enumerate(heads)]
        project_late(i)
        if scores:
            rows = chunk_rows(i)
            dec = dec_scr.at[par]
            q = proj_scr[rows, _Q0:_Q0 + KDIM]
            v = proj_scr[rows, _I0:_I0 + WIDTH]
            kk = kk_scr[par]
            for hd, sl in enumerate(heads):
                kdec = (kk[:, sl] * dec[C:2 * C, sl]).astype(BF16)
                st_t = state_scr[hd]
                sprev_scr[par, hd] = st_t.astype(BF16)
                state_scr[hd] = st_t * dec[C - 1:C, sl] + _dot(v[:, sl].T.astype(BF16), kdec)
            qb_scr[par] = (q * dec[0:C, :]).astype(BF16)
            scs = []
            for hd, sl in enumerate(heads):
                qh, kh = q[:, sl], kk[:, sl]
                sc = _dot_nt(qh.astype(BF16), kh.astype(BF16))
                for li in range(len(LEVELS)):
                    dl = dec[(2 + li) * C:(3 + li) * C, sl]
                    s_l = _dot_nt((qh * dl).astype(BF16), (kh * dl).astype(BF16))
                    sc = jnp.where(level_masks[li], s_l, sc)
                scs.append(jnp.where(causal, sc, 0.0).astype(BF16))
        if gates:
            fl = proj_scr[chunk_rows(i + 1), _F0:_F0 + KDIM]
            e = jnp.exp(-jnp.abs(fl))
            r = 1.0 / (1.0 + e)
            er = e * r
            nonneg = fl >= 0.0
            sig = jnp.where(nonneg, r, er)
            sig_neg = jnp.where(nonneg, er, r)
            f = lb + oml * sig
            lf = jnp.log2(jnp.maximum(f, LOG_FLOOR))
            kk_scr[prv] = oml * sig_neg
            lf_hi = lf.astype(BF16)
            lf_lo = (lf - lf_hi.astype(F32)).astype(BF16)
            expo = _dot(wexp, jnp.concatenate([lf_hi, lf_lo], axis=0))
            dec_scr[prv] = jnp.exp2(expo)
        if finish:
            rows_f = chunk_rows(i - 1)
            g = proj_scr[rows_f, _G0:_G0 + WIDTH]
            for hd, sl in enumerate(heads):
                o = o_scr[prv, :, sl] + o_inter[hd]
                o = o * lax.rsqrt(jnp.mean(o * o, axis=-1, keepdims=True) + EPS)
                a_scr[rows_f, sl] = o * ogain[:, sl] * _sigmoid(g[:, sl])
        if scores:
            for hd, sl in enumerate(heads):
                o_scr[par, :, sl] = _dot(scs[hd], v[:, sl].astype(BF16))

    step(-1, 1, finish=False, scores=False)
    if nch == 1:
        step(0, 0, finish=False, gates=False)
    else:
        assert nch % 2 == 0
        step(0, 0, finish=False)

        for i in range(1, nch - 1):
            step(i, i % 2)
        step(nch - 1, 1, gates=False)
    step(nch, nch % 2, scores=False, gates=False)

    vp = late_scr[:, 0:POOL_WIDTH]
    pbuf_scr[POOL_HIST:POOL_HIST + T, :] = vp
    ys = []
    for gi, w in enumerate(POOL_WINDOWS):
        cs = slice(gi * POOL_GROUP, (gi + 1) * POOL_GROUP)
        s = pbuf_scr[:, cs]
        span = 1
        while span < w:
            s = s + pltpu.roll(s, span, 0)
            span *= 2
        s = s[POOL_HIST:]
        if meta:
            pos = lax.broadcasted_iota(jnp.int32, (T, 1), 0) - (T - N_META)
            cnt = jnp.clip(pos + 1, 1, w).astype(F32)
            pooled = s / cnt - vp[:, cs]
        else:
            pooled = s * (1.0 / w) - vp[:, cs]
        ys.append(_dot(pooled.astype(BF16), pproj_ref[gi]))
    p = jnp.concatenate(ys, axis=-1) * pscale_ref[...]
    hist = pbuf_scr[T:T + POOL_HIST, :]
    pbuf_scr[0:POOL_HIST, :] = hist

    br_a = _dot(a_scr[...].astype(BF16), wbh_ref[...])
    br_p = _dot(p.astype(BF16), wbp_ref[...])
    z = (_sigmoid(late_scr[:, _A0 - _P0:_B0 - _P0]) * br_a
         + _sigmoid(late_scr[:, _B0 - _P0:IN_COLS - _P0]) * br_p)
    zz = _dot(z.astype(BF16), wout_ref[...])
    o_ref[...] = h + _rms(zz, gpost_ref[...])

    if meta:
        st_out_ref[...] = state_scr[...]
        ph_out_ref[...] = hist


def _ffn_kernel(h_ref, c0_ref, gpre_ref, wg_ref, wu_ref, cw_ref, cb_ref, wd_ref, gpost_ref, *rest,
                tile, meta):
    if meta:
        o_ref, c_out_ref, hist_scr, act_scr = rest
    else:
        o_ref, hist_scr, act_scr = rest
    T = tile

    @pl.when(pl.program_id(1) == 0)
    def _():
        hist_scr[...] = c0_ref[...]

    h = h_ref[...]
    u = _rms(h, gpre_ref[...]).astype(BF16)
    row8 = lax.broadcasted_iota(jnp.int32, (SUBLANES, FF_BLOCK), 0)

    def shifted(g, prev, k):
        rolled = pltpu.roll(g, k, 0)
        top = jnp.where(row8 < k, pltpu.roll(prev, k, 0), rolled[0:SUBLANES])
        return jnp.concatenate([top, rolled[SUBLANES:]], axis=0)

    for j in range(D_FF // FF_BLOCK):
        cols = slice(j * FF_BLOCK, (j + 1) * FF_BLOCK)
        g = _dot(u, wg_ref[:, cols])
        up = _dot(u, wu_ref[:, cols])
        prev = hist_scr[:, cols]
        hist_scr[:, cols] = g[T - SUBLANES:T]
        cw = cw_ref[:, cols]
        gc = cb_ref[:, cols] + shifted(g, prev, 2) * cw[0:1] + shifted(g, prev, 1) * cw[1:2] + g * cw[2:3]
        e = jnp.exp2(gc * (_GELU_K1 + _GELU_K3 * (gc * gc)))
        act_scr[:, cols] = ((gc * up) * (1.0 / (1.0 + e))).astype(BF16)

    y = _dot(act_scr[...], wd_ref[...])
    o_ref[...] = h + _rms(y, gpost_ref[...])
    if meta:
        c_out_ref[...] = hist_scr[...]


class _Layer:
    def __init__(self, stacked, layer):
        self.array, self.layer = stacked, layer


def _const_spec(operand):
    if isinstance(operand, _Layer):
        shape, layer = operand.array.shape[1:], operand.layer
        return pl.BlockSpec((None,) + shape, lambda b, t: (layer,) + (0,) * len(shape),
                            pipeline_mode=pl.Buffered(1))
    nd = len(operand.shape)
    return pl.BlockSpec(operand.shape, lambda b, t: (0,) * nd, pipeline_mode=pl.Buffered(1))


def _arrays(operands):
    return [o.array if isinstance(o, _Layer) else o for o in operands]


def _mixer(h, st0, ph0, weights, wexp, *, tile, layer, meta):
    bsz, seq, _ = h.shape
    nt = seq // tile
    tok_spec = pl.BlockSpec((None, tile, D_MODEL), lambda b, t: (b, t, 0))
    consts = (st0, ph0) + tuple(weights) + (wexp,)
    out_shape = [jax.ShapeDtypeStruct(h.shape, F32)]
    out_specs = [tok_spec]
    if meta:
        out_shape += [jax.ShapeDtypeStruct(st0.shape, F32), jax.ShapeDtypeStruct(ph0.shape, F32)]
        out_specs += [_const_spec(st0), _const_spec(ph0)]
    out = pl.pallas_call(
        functools.partial(_mixer_kernel, tile=tile, layer=layer, meta=meta),
        grid=(bsz, nt),
        in_specs=[tok_spec] + [_const_spec(c) for c in consts],
        out_specs=out_specs,
        out_shape=out_shape,
        scratch_shapes=[
            pltpu.VMEM((tile, _P0), F32),
            pltpu.VMEM((tile, IN_COLS - _P0), F32),
            pltpu.VMEM((tile, D_MODEL), BF16),
            pltpu.VMEM((HEADS, HEAD_V, HEAD_K), F32),
            pltpu.VMEM((POOL_HIST + tile, POOL_WIDTH), F32),
            pltpu.VMEM((tile, WIDTH), F32),
            pltpu.VMEM((2, CHUNK, WIDTH), F32),
            pltpu.VMEM((2, CHUNK, KDIM), BF16),
            pltpu.VMEM((2, CHUNK, KDIM), F32),
            pltpu.VMEM((2, (2 + len(LEVELS)) * CHUNK, KDIM), F32),
            pltpu.VMEM((2, HEADS, HEAD_V, HEAD_K), BF16),
        ],
        compiler_params=pltpu.CompilerParams(
            dimension_semantics=("arbitrary", "arbitrary"), vmem_limit_bytes=VMEM_LIMIT_BYTES),
        name="mixer_meta" if meta else "mixer",
    )(h, *_arrays(consts))
    return out if meta else out[0]


def _ffn(h, c0, weights, *, tile, meta):
    bsz, seq, _ = h.shape
    nt = seq // tile
    tok_spec = pl.BlockSpec((None, tile, D_MODEL), lambda b, t: (b, t, 0))
    consts = (c0,) + tuple(weights)
    out_shape = [jax.ShapeDtypeStruct(h.shape, F32)]
    out_specs = [tok_spec]
    if meta:
        out_shape += [jax.ShapeDtypeStruct(c0.shape, F32)]
        out_specs += [_const_spec(c0)]
    out = pl.pallas_call(
        functools.partial(_ffn_kernel, tile=tile, meta=meta),
        grid=(bsz, nt),
        in_specs=[tok_spec] + [_const_spec(c) for c in consts],
        out_specs=out_specs,
        out_shape=out_shape,
        scratch_shapes=[
            pltpu.VMEM((SUBLANES, D_FF), F32),
            pltpu.VMEM((tile, D_FF), BF16),
        ],
        compiler_params=pltpu.CompilerParams(
            dimension_semantics=("arbitrary", "arbitrary"), vmem_limit_bytes=VMEM_LIMIT_BYTES),
        name="ffn_meta" if meta else "ffn",
    )(h, *_arrays(consts))
    return out if meta else out[0]


def _pick_tile(seq):
    for t in (512, 256, 128, 64):
        if seq % t == 0:
            return t
    raise ValueError(f"sequence length {seq} must be a multiple of {CHUNK}")


def kernel(x, meta_tokens, mix_norm_pre, mix_norm_post, w_in, hgrn_lower_bounds, hgrn_out_norm, w_branch_hgrn, pool_proj, pool_scale, w_branch_pool, w_out, ffn_norm_pre, ffn_norm_post, ffn_w_gate, ffn_w_up, ffn_conv_w, ffn_conv_b, ffn_w_down):
    bsz, seq, d = x.shape
    assert d == D_MODEL and meta_tokens.shape == (N_META, D_MODEL)
    tile = _pick_tile(seq)
    wexp = jnp.asarray(_decay_sum_matrix(), BF16)
    row = lambda a: a.reshape(1, -1).astype(F32)

    h = x.astype(F32)
    hm = jnp.concatenate([jnp.zeros((CHUNK - N_META, D_MODEL), F32), meta_tokens.astype(F32)], axis=0)[None]
    st_zero = jnp.zeros((HEADS, HEAD_V, HEAD_K), F32)
    ph_zero = jnp.zeros((POOL_HIST, POOL_WIDTH), F32)
    cv_zero = jnp.zeros((SUBLANES, D_FF), F32)
    lbraw = hgrn_lower_bounds.astype(F32)

    bf = lambda a: a.astype(BF16)
    w_in_b, w_bh_b, pproj_b, w_bp_b, w_out_b = bf(w_in), bf(w_branch_hgrn), bf(pool_proj), bf(w_branch_pool), bf(w_out)
    w_gate_b, w_up_b, w_down_b = bf(ffn_w_gate), bf(ffn_w_up), bf(ffn_w_down)
    conv_w = ffn_conv_w.astype(F32)

    for l in range(DEPTH):
        mix_w = (row(mix_norm_pre[l]), _Layer(w_in_b, l), lbraw, row(hgrn_out_norm[l]),
                 _Layer(w_bh_b, l), _Layer(pproj_b, l), row(pool_scale[l]),
                 _Layer(w_bp_b, l), _Layer(w_out_b, l), row(mix_norm_post[l]))
        ffn_w = (row(ffn_norm_pre[l]), _Layer(w_gate_b, l), _Layer(w_up_b, l),
                 _Layer(conv_w, l), row(ffn_conv_b[l]), _Layer(w_down_b, l),
                 row(ffn_norm_post[l]))
        hm, st, ph = _mixer(hm, st_zero, ph_zero, mix_w, wexp, tile=CHUNK, layer=l, meta=True)
        h = _mixer(h, st, ph, mix_w, wexp, tile=tile, layer=l, meta=False)
        hm, cv = _ffn(hm, cv_zero, ffn_w, tile=CHUNK, meta=True)
        h = _ffn(h, cv, ffn_w, tile=tile, meta=False)
    return h.astype(x.dtype)
```

```python
import functools
import math

import numpy as np
import jax
import jax.numpy as jnp
from jax import lax
from jax.experimental import pallas as pl
from jax.experimental.pallas import tpu as pltpu

D_MODEL = 1024
DEPTH = 2
N_META = 16
HEADS = 4
HEAD_K = 128
HEAD_V = 128
KDIM = HEADS * HEAD_K
WIDTH = HEADS * HEAD_V
CHUNK = 64
POOL_WINDOWS = (2, 4, 8, 16)
POOL_GROUP = 128
POOL_WIDTH = POOL_GROUP * len(POOL_WINDOWS)
POOL_HIST = 16
D_FF = 2816
FF_BLOCK = 256
PROJ_BLOCK = 256
CONV_WIDTH = 3
SUBLANES = 8
EPS = 1e-6
LOG_FLOOR = 1e-30
IN_COLS = 2 * KDIM + 2 * WIDTH + POOL_WIDTH + 2 * D_MODEL

_Q0, _F0, _I0, _G0, _P0, _A0, _B0 = (0, KDIM, 2 * KDIM, 2 * KDIM + WIDTH, 2 * KDIM + 2 * WIDTH,
                                     2 * KDIM + 2 * WIDTH + POOL_WIDTH,
                                     2 * KDIM + 2 * WIDTH + POOL_WIDTH + D_MODEL)

LEVELS = (32, 16, 8, 4, 2, 1)
MXU_LEVELS = tuple(c for c in LEVELS if c < SUBLANES)
VMEM_LIMIT_BYTES = 60 * 1024 * 1024

_GELU_K1 = -2.0 * math.sqrt(2.0 / math.pi) * math.log2(math.e)
_GELU_K3 = _GELU_K1 * 0.044715

F32 = jnp.float32
BF16 = jnp.bfloat16


def _decay_sum_matrix():
    c64 = CHUNK
    w = np.zeros(((1 + len(MXU_LEVELS)) * c64, c64), np.float32)
    for t in range(c64):
        w[t, :t + 1] = 1.0
    for li, c in enumerate(MXU_LEVELS):
        base = (1 + li) * c64
        for r in range(c64):
            mid = r - r % (2 * c) + c - 1
            if r > mid:
                w[base + r, mid + 1:r + 1] = 1.0
            else:
                w[base + r, r + 1:mid + 1] = 1.0
    return np.concatenate([w, w], axis=1)


def _rms(x, gain):
    ms = jnp.mean(x * x, axis=-1, keepdims=True)
    return x * lax.rsqrt(ms + EPS) * gain


def _sigmoid(x):
    return 1.0 / (1.0 + jnp.exp(-x))


def _dot(a, b):
    return jnp.dot(a, b, preferred_element_type=F32)


def _dot_nt(a, b):
    return lax.dot_general(a, b, (((1,), (1,)), ((), ())), preferred_element_type=F32)


def _lower_bound(lbraw, layer):
    rows = [lbraw[j:j + 1, :] for j in range(DEPTH)]
    m = functools.reduce(jnp.maximum, rows)
    es = [jnp.exp(r - m) for r in rows]
    tot = functools.reduce(lambda a, b: a + b, es)
    gam = [e / tot for e in es]
    cum = functools.reduce(lambda a, b: a + b, gam[:layer + 1])
    return jnp.clip(cum - gam[0], 0.0, 1.0)


def _mixer_kernel(h_ref, st0_ref, ph0_ref, gpre_ref, win_ref, lbraw_ref, ogain_ref, wbh_ref, pproj_ref,
                  pscale_ref, wbp_ref, wout_ref, gpost_ref, wexp_ref, *rest, tile, layer, meta):
    if meta:
        o_ref, st_out_ref, ph_out_ref, *scratch = rest
    else:
        o_ref, *scratch = rest
    proj_scr, late_scr, u_scr, state_scr, pbuf_scr, a_scr, o_scr, qb_scr, kk_scr, dec_scr, decl_scr, sprev_scr = scratch
    T = tile
    C = CHUNK

    @pl.when(pl.program_id(1) == 0)
    def _():
        state_scr[...] = st0_ref[...]
        pbuf_scr[0:POOL_HIST, :] = ph0_ref[...]

    h = h_ref[...]
    u_scr[...] = _rms(h, gpre_ref[...]).astype(BF16)
    nch = T // C
    proj_scr[...] = _dot(u_scr[...], win_ref[:, 0:_P0])
    late_blocks = (IN_COLS - _P0) // PROJ_BLOCK
    per_step = -(-late_blocks // (nch + 2))

    def project_late(i):
        first = (i + 1) * per_step
        count = per_step if not isinstance(i, int) else max(0, min(per_step, late_blocks - first))
        for b in range(count):
            start = (first + b) * PROJ_BLOCK
            if not isinstance(start, int):
                start = pl.multiple_of(start, PROJ_BLOCK)
            late_scr[:, pl.ds(start, PROJ_BLOCK)] = _dot(u_scr[...], win_ref[:, pl.ds(_P0 + start, PROJ_BLOCK)])

    lb = _lower_bound(lbraw_ref[...], layer)
    oml = 1.0 - lb
    ogain = ogain_ref[...]
    wexp = wexp_ref[...]
    ti = lax.broadcasted_iota(jnp.int32, (C, C), 0)
    si = lax.broadcasted_iota(jnp.int32, (C, C), 1)
    xor = ti ^ si
    level_masks = [(xor >= c) & (xor < 2 * c) for c in LEVELS]
    causal = ti >= si

    def chunk_rows(ci):
        return pl.ds(ci * C if isinstance(ci, int) else pl.multiple_of(ci * C, C), C)

    heads = [slice(hd * HEAD_K, (hd + 1) * HEAD_K) for hd in range(HEADS)]

    def step(i, par, finish=True, scores=True, gates=True):
        prv = 1 - par
        if finish:
            o_inter = [_dot_nt(qb_scr[prv, :, sl], sprev_scr[prv, hd]) for hd, sl in enumerate(heads)]
        project_late(i)
        if scores:
            rows = chunk_rows(i)
            dec = dec_scr.at[par]
            q = proj_scr[rows, _Q0:_Q0 + KDIM]
            v = proj_scr[rows, _I0:_I0 + WIDTH]
            kk = kk_scr[par]
            for hd, sl in enumerate(heads):
                kdec = (kk[:, sl] * dec[C:2 * C, sl]).astype(BF16)
                st_t = state_scr[hd]
                sprev_scr[par, hd] = st_t.astype(BF16)
                state_scr[hd] = st_t * dec[C - 1:C, sl] + _dot(v[:, sl].T.astype(BF16), kdec)
            qb_scr[par] = (q * dec[0:C, :]).astype(BF16)
            scs = []
            q16, k16 = q.astype(BF16), kk.astype(BF16)
            for hd, sl in enumerate(heads):
                qh, kh = q16[:, sl], k16[:, sl]
                sc = _dot_nt(qh, kh)
                for li in range(len(LEVELS)):
                    dl = decl_scr[par, li * C:(li + 1) * C, sl]
                    s_l = _dot_nt(qh * dl, kh * dl)
                    sc = jnp.where(level_masks[li], s_l, sc)
                scs.append(jnp.where(causal, sc, 0.0).astype(BF16))
        if gates:
            fl = proj_scr[chunk_rows(i + 1), _F0:_F0 + KDIM]
            e = jnp.exp(-jnp.abs(fl))
            r = 1.0 / (1.0 + e)
            er = e * r
            nonneg = fl >= 0.0
            sig = jnp.where(nonneg, r, er)
            sig_neg = jnp.where(nonneg, er, r)
            f = lb + oml * sig
            lf = jnp.log2(jnp.maximum(f, LOG_FLOOR))
            kk_scr[prv] = oml * sig_neg
            lf_hi = lf.astype(BF16)
            lf_lo = (lf - lf_hi.astype(F32)).astype(BF16)
            expo = _dot(wexp, jnp.concatenate([lf_hi, lf_lo], axis=0))
            b = expo[0:C, :]
            dec_scr[prv, 0:C, :] = jnp.exp2(b)
            dec_scr[prv, C:2 * C, :] = jnp.exp2(b[C - 1:C, :] - b)
            row = 0
            for c in LEVELS:
                if c in MXU_LEVELS:
                    k = 1 + MXU_LEVELS.index(c)
                    blk = expo[k * C:(k + 1) * C, :]
                else:
                    parts = []
                    for r0 in range(0, C, 2 * c):
                        mid = b[r0 + c - 1:r0 + c, :]
                        parts += [mid - b[r0:r0 + c, :], b[r0 + c:r0 + 2 * c, :] - mid]
                    blk = jnp.concatenate(parts, axis=0)
                decl_scr[prv, row * C:(row + 1) * C, :] = jnp.exp2(blk.astype(BF16))
                row += 1
        if finish:
            rows_f = chunk_rows(i - 1)
            g = proj_scr[rows_f, _G0:_G0 + WIDTH]
            for hd, sl in enumerate(heads):
                o = o_scr[prv, :, sl] + o_inter[hd]
                o = o * lax.rsqrt(jnp.mean(o * o, axis=-1, keepdims=True) + EPS)
                a_scr[rows_f, sl] = o * ogain[:, sl] * _sigmoid(g[:, sl])
        if scores:
            for hd, sl in enumerate(heads):
                o_scr[par, :, sl] = _dot(scs[hd], v[:, sl].astype(BF16))

    step(-1, 1, finish=False, scores=False)
    if nch == 1:
        step(0, 0, finish=False, gates=False)
    else:
        assert nch % 2 == 0
        step(0, 0, finish=False)

        for i in range(1, nch - 1):
            step(i, i % 2)
        step(nch - 1, 1, gates=False)
    step(nch, nch % 2, scores=False, gates=False)

    vp = late_scr[:, 0:POOL_WIDTH]
    pbuf_scr[POOL_HIST:POOL_HIST + T, :] = vp
    ys = []
    for gi, w in enumerate(POOL_WINDOWS):
        cs = slice(gi * POOL_GROUP, (gi + 1) * POOL_GROUP)
        s = pbuf_scr[:, cs]
        span = 1
        while span < w:
            s = s + pltpu.roll(s, span, 0)
            span *= 2
        s = s[POOL_HIST:]
        if meta:
            pos = lax.broadcasted_iota(jnp.int32, (T, 1), 0) - (T - N_META)
            cnt = jnp.clip(pos + 1, 1, w).astype(F32)
            pooled = s / cnt - vp[:, cs]
        else:
            pooled = s * (1.0 / w) - vp[:, cs]
        ys.append(_dot(pooled.astype(BF16), pproj_ref[gi]))
    p = jnp.concatenate(ys, axis=-1) * pscale_ref[...]
    hist = pbuf_scr[T:T + POOL_HIST, :]
    pbuf_scr[0:POOL_HIST, :] = hist

    br_a = _dot(a_scr[...].astype(BF16), wbh_ref[...])
    br_p = _dot(p.astype(BF16), wbp_ref[...])
    z = (_sigmoid(late_scr[:, _A0 - _P0:_B0 - _P0]) * br_a
         + _sigmoid(late_scr[:, _B0 - _P0:IN_COLS - _P0]) * br_p)
    zz = _dot(z.astype(BF16), wout_ref[...])
    o_ref[...] = h + _rms(zz, gpost_ref[...])

    if meta:
        st_out_ref[...] = state_scr[...]
        ph_out_ref[...] = hist


def _ffn_kernel(h_ref, c0_ref, gpre_ref, wg_ref, wu_ref, cw_ref, cb_ref, wd_ref, gpost_ref, *rest,
                tile, rows, meta):
    if meta:
        o_ref, c_out_ref, hist_scr, act_scr = rest
    else:
        o_ref, hist_scr, act_scr = rest
    T = tile

    @pl.when(pl.program_id(1) == 0)
    def _():
        for r in range(rows):
            hist_scr[r] = c0_ref[...]

    row8 = lax.broadcasted_iota(jnp.int32, (SUBLANES, FF_BLOCK), 0)

    def shifted(g, prev, k):
        rolled = pltpu.roll(g, k, 0)
        top = jnp.where(row8 < k, pltpu.roll(prev, k, 0), rolled[0:SUBLANES])
        return jnp.concatenate([top, rolled[SUBLANES:]], axis=0)

    for r in range(rows):
        h = h_ref[r]
        u = _rms(h, gpre_ref[...]).astype(BF16)
        for j in range(D_FF // FF_BLOCK):
            cols = slice(j * FF_BLOCK, (j + 1) * FF_BLOCK)
            g = _dot(u, wg_ref[:, cols])
            up = _dot(u, wu_ref[:, cols])
            prev = hist_scr[r, :, cols]
            hist_scr[r, :, cols] = g[T - SUBLANES:T]
            cw = cw_ref[:, cols]
            gc = cb_ref[:, cols] + shifted(g, prev, 2) * cw[0:1] + shifted(g, prev, 1) * cw[1:2] + g * cw[2:3]
            e = jnp.exp2(gc * (_GELU_K1 + _GELU_K3 * (gc * gc)))
            act_scr[r, :, cols] = ((gc * up) * (1.0 / (1.0 + e))).astype(BF16)
        y = _dot(act_scr[r], wd_ref[...])
        o_ref[r] = h + _rms(y, gpost_ref[...])
    if meta:
        c_out_ref[...] = hist_scr[0]


class _Layer:
    def __init__(self, stacked, layer):
        self.array, self.layer = stacked, layer


def _const_spec(operand):
    if isinstance(operand, _Layer):
        shape, layer = operand.array.shape[1:], operand.layer
        return pl.BlockSpec((None,) + shape, lambda b, t: (layer,) + (0,) * len(shape),
                            pipeline_mode=pl.Buffered(1))
    nd = len(operand.shape)
    return pl.BlockSpec(operand.shape, lambda b, t: (0,) * nd, pipeline_mode=pl.Buffered(1))


def _arrays(operands):
    return [o.array if isinstance(o, _Layer) else o for o in operands]


def _mixer(h, st0, ph0, weights, wexp, *, tile, layer, meta):
    bsz, seq, _ = h.shape
    nt = seq // tile
    tok_spec = pl.BlockSpec((None, tile, D_MODEL), lambda b, t: (b, t, 0))
    consts = (st0, ph0) + tuple(weights) + (wexp,)
    out_shape = [jax.ShapeDtypeStruct(h.shape, F32)]
    out_specs = [tok_spec]
    if meta:
        out_shape += [jax.ShapeDtypeStruct(st0.shape, F32), jax.ShapeDtypeStruct(ph0.shape, F32)]
        out_specs += [_const_spec(st0), _const_spec(ph0)]
    out = pl.pallas_call(
        functools.partial(_mixer_kernel, tile=tile, layer=layer, meta=meta),
        grid=(bsz, nt),
        in_specs=[tok_spec] + [_const_spec(c) for c in consts],
        out_specs=out_specs,
        out_shape=out_shape,
        scratch_shapes=[
            pltpu.VMEM((tile, _P0), F32),
            pltpu.VMEM((tile, IN_COLS - _P0), F32),
            pltpu.VMEM((tile, D_MODEL), BF16),
            pltpu.VMEM((HEADS, HEAD_V, HEAD_K), F32),
            pltpu.VMEM((POOL_HIST + tile, POOL_WIDTH), F32),
            pltpu.VMEM((tile, WIDTH), F32),
            pltpu.VMEM((2, CHUNK, WIDTH), F32),
            pltpu.VMEM((2, CHUNK, KDIM), BF16),
            pltpu.VMEM((2, CHUNK, KDIM), F32),
            pltpu.VMEM((2, 2 * CHUNK, KDIM), F32),
            pltpu.VMEM((2, len(LEVELS) * CHUNK, KDIM), BF16),
            pltpu.VMEM((2, HEADS, HEAD_V, HEAD_K), BF16),
        ],
        compiler_params=pltpu.CompilerParams(
            dimension_semantics=("arbitrary", "arbitrary"), vmem_limit_bytes=VMEM_LIMIT_BYTES),
        name="mixer_meta" if meta else "mixer",
    )(h, *_arrays(consts))
    return out if meta else out[0]


def _ffn(h, c0, weights, *, tile, meta):
    bsz, seq, _ = h.shape
    nt = seq // tile
    rows = 2 if bsz % 2 == 0 else 1
    tok_spec = pl.BlockSpec((rows, tile, D_MODEL), lambda b, t: (b, t, 0))
    consts = (c0,) + tuple(weights)
    out_shape = [jax.ShapeDtypeStruct(h.shape, F32)]
    out_specs = [tok_spec]
    if meta:
        out_shape += [jax.ShapeDtypeStruct(c0.shape, F32)]
        out_specs += [_const_spec(c0)]
    out = pl.pallas_call(
        functools.partial(_ffn_kernel, tile=tile, rows=rows, meta=meta),
        grid=(bsz // rows, nt),
        in_specs=[tok_spec] + [_const_spec(c) for c in consts],
        out_specs=out_specs,
        out_shape=out_shape,
        scratch_shapes=[
            pltpu.VMEM((rows, SUBLANES, D_FF), F32),
            pltpu.VMEM((rows, tile, D_FF), BF16),
        ],
        compiler_params=pltpu.CompilerParams(
            dimension_semantics=("arbitrary", "arbitrary"), vmem_limit_bytes=VMEM_LIMIT_BYTES),
        name="ffn_meta" if meta else "ffn",
    )(h, *_arrays(consts))
    return out if meta else out[0]


def _pick_tile(seq):
    for t in (512, 256, 128, 64):
        if seq % t == 0:
            return t
    raise ValueError(f"sequence length {seq} must be a multiple of {CHUNK}")


def kernel(x, meta_tokens, mix_norm_pre, mix_norm_post, w_in, hgrn_lower_bounds, hgrn_out_norm, w_branch_hgrn, pool_proj, pool_scale, w_branch_pool, w_out, ffn_norm_pre, ffn_norm_post, ffn_w_gate, ffn_w_up, ffn_conv_w, ffn_conv_b, ffn_w_down):
    bsz, seq, d = x.shape
    assert d == D_MODEL and meta_tokens.shape == (N_META, D_MODEL)
    tile = _pick_tile(seq)
    wexp = jnp.asarray(_decay_sum_matrix(), BF16)
    row = lambda a: a.reshape(1, -1).astype(F32)

    h = x.astype(F32)
    hm = jnp.concatenate([jnp.zeros((CHUNK - N_META, D_MODEL), F32), meta_tokens.astype(F32)], axis=0)[None]
    st_zero = jnp.zeros((HEADS, HEAD_V, HEAD_K), F32)
    ph_zero = jnp.zeros((POOL_HIST, POOL_WIDTH), F32)
    cv_zero = jnp.zeros((SUBLANES, D_FF), F32)
    lbraw = hgrn_lower_bounds.astype(F32)

    bf = lambda a: a.astype(BF16)
    w_in_b, w_bh_b, pproj_b, w_bp_b, w_out_b = bf(w_in), bf(w_branch_hgrn), bf(pool_proj), bf(w_branch_pool), bf(w_out)
    w_gate_b, w_up_b, w_down_b = bf(ffn_w_gate), bf(ffn_w_up), bf(ffn_w_down)
    conv_w = ffn_conv_w.astype(F32)

    for l in range(DEPTH):
        mix_w = (row(mix_norm_pre[l]), _Layer(w_in_b, l), lbraw, row(hgrn_out_norm[l]),
                 _Layer(w_bh_b, l), _Layer(pproj_b, l), row(pool_scale[l]),
                 _Layer(w_bp_b, l), _Layer(w_out_b, l), row(mix_norm_post[l]))
        ffn_w = (row(ffn_norm_pre[l]), _Layer(w_gate_b, l), _Layer(w_up_b, l),
                 _Layer(conv_w, l), row(ffn_conv_b[l]), _Layer(w_down_b, l),
                 row(ffn_norm_post[l]))
        hm, st, ph = _mixer(hm, st_zero, ph_zero, mix_w, wexp, tile=CHUNK, layer=l, meta=True)
        h = _mixer(h, st, ph, mix_w, wexp, tile=tile, layer=l, meta=False)
        hm, cv = _ffn(hm, cv_zero, ffn_w, tile=CHUNK, meta=True)
        h = _ffn(h, cv, ffn_w, tile=tile, meta=False)
    return h.astype(x.dtype)
```

```python
import functools
import math

import numpy as np
import jax
import jax.numpy as jnp
from jax import lax
from jax.experimental import pallas as pl
from jax.experimental.pallas import tpu as pltpu

D_MODEL = 1024
DEPTH = 2
N_META = 16
HEADS = 4
HEAD_K = 128
HEAD_V = 128
KDIM = HEADS * HEAD_K
WIDTH = HEADS * HEAD_V
CHUNK = 64
POOL_WINDOWS = (2, 4, 8, 16)
POOL_GROUP = 128
POOL_WIDTH = POOL_GROUP * len(POOL_WINDOWS)
POOL_HIST = 16
D_FF = 2816
FF_BLOCK = 256
PROJ_BLOCK = 256
CONV_WIDTH = 3
SUBLANES = 8
EPS = 1e-6
LOG_FLOOR = 1e-30
IN_COLS = 2 * KDIM + 2 * WIDTH + POOL_WIDTH + 2 * D_MODEL

_Q0, _F0, _I0, _G0, _P0, _A0, _B0 = (0, KDIM, 2 * KDIM, 2 * KDIM + WIDTH, 2 * KDIM + 2 * WIDTH,
                                     2 * KDIM + 2 * WIDTH + POOL_WIDTH,
                                     2 * KDIM + 2 * WIDTH + POOL_WIDTH + D_MODEL)

LEVELS = (32, 16, 8, 4, 2, 1)
VMEM_LIMIT_BYTES = 60 * 1024 * 1024
MIXER_TILE = 512
FFN_TILE = 1024

_GELU_K1 = -2.0 * math.sqrt(2.0 / math.pi) * math.log2(math.e)
_GELU_K3 = _GELU_K1 * 0.044715

F32 = jnp.float32
BF16 = jnp.bfloat16


def _decay_sum_matrix():
    c64 = CHUNK
    w = np.zeros(((2 + len(LEVELS)) * c64, c64), np.float32)
    for t in range(c64):
        w[t, :t + 1] = 1.0
        w[c64 + t, t + 1:] = 1.0
    for li, c in enumerate(LEVELS):
        base = (2 + li) * c64
        for r in range(c64):
            mid = r - r % (2 * c) + c - 1
            if r > mid:
                w[base + r, mid + 1:r + 1] = 1.0
            else:
                w[base + r, r + 1:mid + 1] = 1.0
    return np.concatenate([w, w], axis=1)


def _rms(x, gain):
    ms = jnp.mean(x * x, axis=-1, keepdims=True)
    return x * lax.rsqrt(ms + EPS) * gain


def _sigmoid(x):
    return 1.0 / (1.0 + jnp.exp(-x))


def _dot(a, b):
    return jnp.dot(a, b, preferred_element_type=F32)


def _dot_nt(a, b):
    return lax.dot_general(a, b, (((1,), (1,)), ((), ())), preferred_element_type=F32)


def _lower_bound(lbraw, layer):
    rows = [lbraw[j:j + 1, :] for j in range(DEPTH)]
    m = functools.reduce(jnp.maximum, rows)
    es = [jnp.exp(r - m) for r in rows]
    tot = functools.reduce(lambda a, b: a + b, es)
    gam = [e / tot for e in es]
    cum = functools.reduce(lambda a, b: a + b, gam[:layer + 1])
    return jnp.clip(cum - gam[0], 0.0, 1.0)


def _mixer_kernel(h_ref, st0_ref, ph0_ref, gpre_ref, win_ref, lbraw_ref, ogain_ref, wbh_ref, pproj_ref,
                  pscale_ref, wbp_ref, wout_ref, gpost_ref, wexp_ref, *rest, tile, layer, meta):
    if meta:
        o_ref, st_out_ref, ph_out_ref, *scratch = rest
    else:
        o_ref, *scratch = rest
    proj_scr, late_scr, u_scr, state_scr, pbuf_scr, a_scr, o_scr, qb_scr, kk_scr, dec_scr, sprev_scr = scratch
    T = tile
    C = CHUNK

    @pl.when(pl.program_id(1) == 0)
    def _():
        state_scr[...] = st0_ref[...]
        pbuf_scr[0:POOL_HIST, :] = ph0_ref[...]

    h = h_ref[...]
    u_scr[...] = _rms(h, gpre_ref[...]).astype(BF16)
    nch = T // C
    proj_scr[...] = _dot(u_scr[...], win_ref[:, 0:_P0])
    late_blocks = (IN_COLS - _P0) // PROJ_BLOCK
    per_step = -(-late_blocks // (nch + 2))

    def project_late(i):
        first = (i + 1) * per_step
        for b in range(max(0, min(per_step, late_blocks - first))):
            start = (first + b) * PROJ_BLOCK
            late_scr[:, start:start + PROJ_BLOCK] = _dot(
                u_scr[...], win_ref[:, _P0 + start:_P0 + start + PROJ_BLOCK])

    lb = _lower_bound(lbraw_ref[...], layer)
    oml = 1.0 - lb
    ogain = ogain_ref[...]
    wexp = wexp_ref[...]
    ti = lax.broadcasted_iota(jnp.int32, (C, C), 0)
    si = lax.broadcasted_iota(jnp.int32, (C, C), 1)
    xor = ti ^ si
    level_masks = [(xor >= c) & (xor < 2 * c) for c in LEVELS]
    causal = ti >= si

    def chunk_rows(ci):
        return slice(ci * C, (ci + 1) * C)

    heads = [slice(hd * HEAD_K, (hd + 1) * HEAD_K) for hd in range(HEADS)]

    def step(i, par, finish=True, scores=True, gates=True):
        prv = 1 - par
        if finish:
            o_inter = [_dot_nt(qb_scr[prv, :, sl], sprev_scr[prv, hd]) for hd, sl in enumerate(heads)]
        project_late(i)
        if scores:
            rows = chunk_rows(i)
            dec = dec_scr.at[par]
            q = proj_scr[rows, _Q0:_Q0 + KDIM]
            v = proj_scr[rows, _I0:_I0 + WIDTH]
            kk = kk_scr[par]
            for hd, sl in enumerate(heads):
                kdec = (kk[:, sl] * dec[C:2 * C, sl]).astype(BF16)
                st_t = state_scr[hd]
                sprev_scr[par, hd] = st_t.astype(BF16)
                state_scr[hd] = st_t * dec[C - 1:C, sl] + _dot(v[:, sl].T.astype(BF16), kdec)
            qb_scr[par] = (q * dec[0:C, :]).astype(BF16)
            scs = []
            for hd, sl in enumerate(heads):
                qh, kh = q[:, sl], kk[:, sl]
                sc = _dot_nt(qh.astype(BF16), kh.astype(BF16))
                for li in range(len(LEVELS)):
                    dl = dec[(2 + li) * C:(3 + li) * C, sl]
                    s_l = _dot_nt((qh * dl).astype(BF16), (kh * dl).astype(BF16))
                    sc = jnp.where(level_masks[li], s_l, sc)
                scs.append(jnp.where(causal, sc, 0.0).astype(BF16))
        if gates:
            fl = proj_scr[chunk_rows(i + 1), _F0:_F0 + KDIM]
            e = jnp.exp(-jnp.abs(fl))
            r = 1.0 / (1.0 + e)
            er = e * r
            nonneg = fl >= 0.0
            sig = jnp.where(nonneg, r, er)
            sig_neg = jnp.where(nonneg, er, r)
            f = lb + oml * sig
            lf = jnp.log2(jnp.maximum(f, LOG_FLOOR))
            kk_scr[prv] = oml * sig_neg
            lf_hi = lf.astype(BF16)
            lf_lo = (lf - lf_hi.astype(F32)).astype(BF16)
            expo = _dot(wexp, jnp.concatenate([lf_hi, lf_lo], axis=0))
            dec_scr[prv] = jnp.exp2(expo)
        if finish:
            rows_f = chunk_rows(i - 1)
            g = proj_scr[rows_f, _G0:_G0 + WIDTH]
            for hd, sl in enumerate(heads):
                o = o_scr[prv, :, sl] + o_inter[hd]
                o = o * lax.rsqrt(jnp.mean(o * o, axis=-1, keepdims=True) + EPS)
                a_scr[rows_f, sl] = o * ogain[:, sl] * _sigmoid(g[:, sl])
        if scores:
            for hd, sl in enumerate(heads):
                o_scr[par, :, sl] = _dot(scs[hd], v[:, sl].astype(BF16))

    step(-1, 1, finish=False, scores=False)
    for i in range(nch):
        step(i, i % 2, finish=i > 0, gates=i < nch - 1)
    step(nch, nch % 2, scores=False, gates=False)

    vp = late_scr[:, 0:POOL_WIDTH]
    pbuf_scr[POOL_HIST:POOL_HIST + T, :] = vp
    ys = []
    for gi, w in enumerate(POOL_WINDOWS):
        cs = slice(gi * POOL_GROUP, (gi + 1) * POOL_GROUP)
        s = pbuf_scr[:, cs]
        span = 1
        while span < w:
            s = s + pltpu.roll(s, span, 0)
            span *= 2
        s = s[POOL_HIST:]
        if meta:
            pos = lax.broadcasted_iota(jnp.int32, (T, 1), 0) - (T - N_META)
            cnt = jnp.clip(pos + 1, 1, w).astype(F32)
            pooled = s / cnt - vp[:, cs]
        else:
            pooled = s * (1.0 / w) - vp[:, cs]
        ys.append(_dot(pooled.astype(BF16), pproj_ref[gi]))
    p = jnp.concatenate(ys, axis=-1) * pscale_ref[...]
    hist = pbuf_scr[T:T + POOL_HIST, :]
    pbuf_scr[0:POOL_HIST, :] = hist

    br_a = _dot(a_scr[...].astype(BF16), wbh_ref[...])
    br_p = _dot(p.astype(BF16), wbp_ref[...])
    z = (_sigmoid(late_scr[:, _A0 - _P0:_B0 - _P0]) * br_a
         + _sigmoid(late_scr[:, _B0 - _P0:IN_COLS - _P0]) * br_p)
    zz = _dot(z.astype(BF16), wout_ref[...])
    o_ref[...] = h + _rms(zz, gpost_ref[...])

    if meta:
        st_out_ref[...] = state_scr[...]
        ph_out_ref[...] = hist


def _ffn_kernel(h_ref, c0_ref, gpre_ref, wg_ref, wu_ref, cw_ref, cb_ref, wd_ref, gpost_ref, *rest,
                tile, meta):
    if meta:
        o_ref, c_out_ref, hist_scr, act_scr = rest
    else:
        o_ref, hist_scr, act_scr = rest
    T = tile

    @pl.when(pl.program_id(1) == 0)
    def _():
        hist_scr[...] = c0_ref[...]

    h = h_ref[...]
    u = _rms(h, gpre_ref[...]).astype(BF16)
    row8 = lax.broadcasted_iota(jnp.int32, (SUBLANES, FF_BLOCK), 0)

    def shifted(g, prev, k):
        rolled = pltpu.roll(g, k, 0)
        top = jnp.where(row8 < k, pltpu.roll(prev, k, 0), rolled[0:SUBLANES])
        return jnp.concatenate([top, rolled[SUBLANES:]], axis=0)

    for j in range(D_FF // FF_BLOCK):
        cols = slice(j * FF_BLOCK, (j + 1) * FF_BLOCK)
        g = _dot(u, wg_ref[:, cols])
        up = _dot(u, wu_ref[:, cols])
        prev = hist_scr[:, cols]
        hist_scr[:, cols] = g[T - SUBLANES:T]
        cw = cw_ref[:, cols]
        gc = cb_ref[:, cols] + shifted(g, prev, 2) * cw[0:1] + shifted(g, prev, 1) * cw[1:2] + g * cw[2:3]
        e = jnp.exp2(gc * (_GELU_K1 + _GELU_K3 * (gc * gc)))
        act_scr[:, cols] = ((gc * up) * (1.0 / (1.0 + e))).astype(BF16)

    y = _dot(act_scr[...], wd_ref[...])
    o_ref[...] = h + _rms(y, gpost_ref[...])
    if meta:
        c_out_ref[...] = hist_scr[...]


class _Layer:
    def __init__(self, stacked, layer):
        self.array, self.layer = stacked, layer


def _const_spec(operand):
    if isinstance(operand, _Layer):
        shape, layer = operand.array.shape[1:], operand.layer
        return pl.BlockSpec((None,) + shape, lambda b, t: (layer,) + (0,) * len(shape),
                            pipeline_mode=pl.Buffered(1))
    nd = len(operand.shape)
    return pl.BlockSpec(operand.shape, lambda b, t: (0,) * nd, pipeline_mode=pl.Buffered(1))


def _arrays(operands):
    return [o.array if isinstance(o, _Layer) else o for o in operands]


def _mixer(h, st0, ph0, weights, wexp, *, tile, layer, meta):
    bsz, seq, _ = h.shape
    nt = seq // tile
    tok_spec = pl.BlockSpec((None, tile, D_MODEL), lambda b, t: (b, t, 0))
    consts = (st0, ph0) + tuple(weights) + (wexp,)
    out_shape = [jax.ShapeDtypeStruct(h.shape, F32)]
    out_specs = [tok_spec]
    if meta:
        out_shape += [jax.ShapeDtypeStruct(st0.shape, F32), jax.ShapeDtypeStruct(ph0.shape, F32)]
        out_specs += [_const_spec(st0), _const_spec(ph0)]
    out = pl.pallas_call(
        functools.partial(_mixer_kernel, tile=tile, layer=layer, meta=meta),
        grid=(bsz, nt),
        in_specs=[tok_spec] + [_const_spec(c) for c in consts],
        out_specs=out_specs,
        out_shape=out_shape,
        scratch_shapes=[
            pltpu.VMEM((tile, _P0), F32),
            pltpu.VMEM((tile, IN_COLS - _P0), F32),
            pltpu.VMEM((tile, D_MODEL), BF16),
            pltpu.VMEM((HEADS, HEAD_V, HEAD_K), F32),
            pltpu.VMEM((POOL_HIST + tile, POOL_WIDTH), F32),
            pltpu.VMEM((tile, WIDTH), F32),
            pltpu.VMEM((2, CHUNK, WIDTH), F32),
            pltpu.VMEM((2, CHUNK, KDIM), BF16),
            pltpu.VMEM((2, CHUNK, KDIM), F32),
            pltpu.VMEM((2, (2 + len(LEVELS)) * CHUNK, KDIM), F32),
            pltpu.VMEM((2, HEADS, HEAD_V, HEAD_K), BF16),
        ],
        compiler_params=pltpu.CompilerParams(
            dimension_semantics=("arbitrary", "arbitrary"), vmem_limit_bytes=VMEM_LIMIT_BYTES),
        name="mixer_meta" if meta else "mixer",
    )(h, *_arrays(consts))
    return out if meta else out[0]


def _ffn(h, c0, weights, *, tile, meta):
    bsz, seq, _ = h.shape
    nt = seq // tile
    tok_spec = pl.BlockSpec((None, tile, D_MODEL), lambda b, t: (b, t, 0))
    consts = (c0,) + tuple(weights)
    out_shape = [jax.ShapeDtypeStruct(h.shape, F32)]
    out_specs = [tok_spec]
    if meta:
        out_shape += [jax.ShapeDtypeStruct(c0.shape, F32)]
        out_specs += [_const_spec(c0)]
    out = pl.pallas_call(
        functools.partial(_ffn_kernel, tile=tile, meta=meta),
        grid=(bsz, nt),
        in_specs=[tok_spec] + [_const_spec(c) for c in consts],
        out_specs=out_specs,
        out_shape=out_shape,
        scratch_shapes=[
            pltpu.VMEM((SUBLANES, D_FF), F32),
            pltpu.VMEM((tile, D_FF), BF16),
        ],
        compiler_params=pltpu.CompilerParams(
            dimension_semantics=("arbitrary", "arbitrary"), vmem_limit_bytes=VMEM_LIMIT_BYTES),
        name="ffn_meta" if meta else "ffn",
    )(h, *_arrays(consts))
    return out if meta else out[0]


def _pick_tile(seq, largest):
    t = largest
    while t >= CHUNK:
        if seq % t == 0:
            return t
        t //= 2
    raise ValueError(f"sequence length {seq} must be a multiple of {CHUNK}")


def kernel(x, meta_tokens, mix_norm_pre, mix_norm_post, w_in, hgrn_lower_bounds, hgrn_out_norm, w_branch_hgrn, pool_proj, pool_scale, w_branch_pool, w_out, ffn_norm_pre, ffn_norm_post, ffn_w_gate, ffn_w_up, ffn_conv_w, ffn_conv_b, ffn_w_down):
    bsz, seq, d = x.shape
    assert d == D_MODEL and meta_tokens.shape == (N_META, D_MODEL)
    mix_tile = _pick_tile(seq, MIXER_TILE)
    ffn_tile = _pick_tile(seq, FFN_TILE)
    wexp = jnp.asarray(_decay_sum_matrix(), BF16)
    row = lambda a: a.reshape(1, -1).astype(F32)

    h = x.astype(F32)
    hm = jnp.concatenate([jnp.zeros((CHUNK - N_META, D_MODEL), F32), meta_tokens.astype(F32)], axis=0)[None]
    st_zero = jnp.zeros((HEADS, HEAD_V, HEAD_K), F32)
    ph_zero = jnp.zeros((POOL_HIST, POOL_WIDTH), F32)
    cv_zero = jnp.zeros((SUBLANES, D_FF), F32)
    lbraw = hgrn_lower_bounds.astype(F32)

    bf = lambda a: a.astype(BF16)
    w_in_b, w_bh_b, pproj_b, w_bp_b, w_out_b = bf(w_in), bf(w_branch_hgrn), bf(pool_proj), bf(w_branch_pool), bf(w_out)
    w_gate_b, w_up_b, w_down_b = bf(ffn_w_gate), bf(ffn_w_up), bf(ffn_w_down)
    conv_w = ffn_conv_w.astype(F32)

    for l in range(DEPTH):
        mix_w = (row(mix_norm_pre[l]), _Layer(w_in_b, l), lbraw, row(hgrn_out_norm[l]),
                 _Layer(w_bh_b, l), _Layer(pproj_b, l), row(pool_scale[l]),
                 _Layer(w_bp_b, l), _Layer(w_out_b, l), row(mix_norm_post[l]))
        ffn_w = (row(ffn_norm_pre[l]), _Layer(w_gate_b, l), _Layer(w_up_b, l),
                 _Layer(conv_w, l), row(ffn_conv_b[l]), _Layer(w_down_b, l),
                 row(ffn_norm_post[l]))
        hm, st, ph = _mixer(hm, st_zero, ph_zero, mix_w, wexp, tile=CHUNK, layer=l, meta=True)
        h = _mixer(h, st, ph, mix_w, wexp, tile=mix_tile, layer=l, meta=False)
        hm, cv = _ffn(hm, cv_zero, ffn_w, tile=CHUNK, meta=True)
        h = _ffn(h, cv, ffn_w, tile=ffn_tile, meta=False)
    return h.astype(x.dtype)
```

```python
import functools
import math

import numpy as np
import jax
import jax.numpy as jnp
from jax import lax
from jax.experimental import pallas as pl
from jax.experimental.pallas import tpu as pltpu

D_MODEL = 1024
DEPTH = 2
N_META = 16
HEADS = 4
HEAD_K = 128
HEAD_V = 128
KDIM = HEADS * HEAD_K
WIDTH = HEADS * HEAD_V
CHUNK = 64
POOL_WINDOWS = (2, 4, 8, 16)
POOL_GROUP = 128
POOL_WIDTH = POOL_GROUP * len(POOL_WINDOWS)
POOL_HIST = 16
D_FF = 2816
FF_BLOCK = 256
PROJ_BLOCK = 256
CONV_WIDTH = 3
SUBLANES = 8
EPS = 1e-6
LOG_FLOOR = 1e-30
IN_COLS = 2 * KDIM + 2 * WIDTH + POOL_WIDTH + 2 * D_MODEL

_Q0, _F0, _I0, _G0, _P0, _A0, _B0 = (0, KDIM, 2 * KDIM, 2 * KDIM + WIDTH, 2 * KDIM + 2 * WIDTH,
                                     2 * KDIM + 2 * WIDTH + POOL_WIDTH,
                                     2 * KDIM + 2 * WIDTH + POOL_WIDTH + D_MODEL)

LEVELS = (32, 16, 8, 4, 2, 1)
MXU_LEVELS = tuple(c for c in LEVELS if c < SUBLANES)
VMEM_LIMIT_BYTES = 60 * 1024 * 1024
MIXER_TILE = 512
FFN_TILE = 1024

_GELU_K1 = -2.0 * math.sqrt(2.0 / math.pi) * math.log2(math.e)
_GELU_K3 = _GELU_K1 * 0.044715

F32 = jnp.float32
BF16 = jnp.bfloat16


def _decay_sum_matrix():
    c64 = CHUNK
    w = np.zeros(((1 + len(MXU_LEVELS)) * c64, c64), np.float32)
    for t in range(c64):
        w[t, :t + 1] = 1.0
    for li, c in enumerate(MXU_LEVELS):
        base = (1 + li) * c64
        for r in range(c64):
            mid = r - r % (2 * c) + c - 1
            if r > mid:
                w[base + r, mid + 1:r + 1] = 1.0
            else:
                w[base + r, r + 1:mid + 1] = 1.0
    return np.concatenate([w, w], axis=1)


def _rms(x, gain):
    ms = jnp.mean(x * x, axis=-1, keepdims=True)
    return x * lax.rsqrt(ms + EPS) * gain


def _sigmoid(x):
    return 1.0 / (1.0 + jnp.exp(-x))


def _dot(a, b):
    return jnp.dot(a, b, preferred_element_type=F32)


def _dot_nt(a, b):
    return lax.dot_general(a, b, (((1,), (1,)), ((), ())), preferred_element_type=F32)


def _block_diag2(a, b):
    za, zb = jnp.zeros_like(b), jnp.zeros_like(a)
    return jnp.concatenate([jnp.concatenate([a, za], axis=1), jnp.concatenate([zb, b], axis=1)], axis=0)


def _block_diag(x):
    n = x.shape[1] // 2
    return _block_diag2(x[:, :n], x[:, n:])


def _lower_bound(lbraw, layer):
    rows = [lbraw[j:j + 1, :] for j in range(DEPTH)]
    m = functools.reduce(jnp.maximum, rows)
    es = [jnp.exp(r - m) for r in rows]
    tot = functools.reduce(lambda a, b: a + b, es)
    gam = [e / tot for e in es]
    cum = functools.reduce(lambda a, b: a + b, gam[:layer + 1])
    return jnp.clip(cum - gam[0], 0.0, 1.0)


def _mixer_kernel(h_ref, st0_ref, ph0_ref, gpre_ref, win_ref, lbraw_ref, ogain_ref, wbh_ref, pproj_ref,
                  pscale_ref, wbp_ref, wout_ref, gpost_ref, wexp_ref, *rest, tile, layer, meta):
    if meta:
        o_ref, st_out_ref, ph_out_ref, *scratch = rest
    else:
        o_ref, *scratch = rest
    proj_scr, late_scr, u_scr, state_scr, pbuf_scr, a_scr, sc_scr, qb_scr, kk_scr, dec_scr, decl_scr, sprev_scr = scratch
    T = tile
    C = CHUNK

    @pl.when(pl.program_id(1) == 0)
    def _():
        state_scr[...] = st0_ref[...]
        pbuf_scr[0:POOL_HIST, :] = ph0_ref[...]

    h = h_ref[...]
    u_scr[...] = _rms(h, gpre_ref[...]).astype(BF16)
    nch = T // C
    proj_scr[...] = _dot(u_scr[...], win_ref[:, 0:_P0])
    late_blocks = (IN_COLS - _P0) // PROJ_BLOCK
    per_step = -(-late_blocks // (nch + 2))

    def project_late(i):
        first = (i + 1) * per_step
        for b in range(max(0, min(per_step, late_blocks - first))):
            start = (first + b) * PROJ_BLOCK
            late_scr[:, start:start + PROJ_BLOCK] = _dot(
                u_scr[...], win_ref[:, _P0 + start:_P0 + start + PROJ_BLOCK])

    lb = _lower_bound(lbraw_ref[...], layer)
    oml = 1.0 - lb
    ogain = ogain_ref[...]
    wexp = wexp_ref[...]
    ti = lax.broadcasted_iota(jnp.int32, (C, 2 * C), 0)
    si = lax.broadcasted_iota(jnp.int32, (C, 2 * C), 1) % C
    xor = ti ^ si
    level_masks = [(xor >= c) & (xor < 2 * c) for c in LEVELS]
    causal = ti >= si

    def chunk_rows(ci):
        return slice(ci * C, (ci + 1) * C)

    heads = [slice(hd * HEAD_K, (hd + 1) * HEAD_K) for hd in range(HEADS)]
    pairs = [slice(pr * 2 * HEAD_K, (pr + 1) * 2 * HEAD_K) for pr in range(HEADS // 2)]

    def step(i, par, finish=True, scores=True, gates=True):
        prv = 1 - par
        if finish:
            rows_f = chunk_rows(i - 1)
            v_f = proj_scr[rows_f, _I0:_I0 + WIDTH].astype(BF16)
            o_pairs = [_dot(sc_scr[prv, pr], _block_diag(v_f[:, sl2]))
                       + _dot_nt(qb_scr[prv, :, sl2],
                                 _block_diag2(sprev_scr[prv, 2 * pr], sprev_scr[prv, 2 * pr + 1]))
                       for pr, sl2 in enumerate(pairs)]
        project_late(i)
        if scores:
            rows = chunk_rows(i)
            dec = dec_scr.at[par]
            q = proj_scr[rows, _Q0:_Q0 + KDIM]
            v = proj_scr[rows, _I0:_I0 + WIDTH]
            kk = kk_scr[par]
            for hd, sl in enumerate(heads):
                kdec = (kk[:, sl] * dec[C:2 * C, sl]).astype(BF16)
                st_t = state_scr[hd]
                sprev_scr[par, hd] = st_t.astype(BF16)
                state_scr[hd] = st_t * dec[C - 1:C, sl] + _dot(v[:, sl].T.astype(BF16), kdec)
            qb_scr[par] = (q * dec[0:C, :]).astype(BF16)
            q16, k16 = q.astype(BF16), kk.astype(BF16)
            for pr, sl2 in enumerate(pairs):
                qp, kp = q16[:, sl2], k16[:, sl2]
                sc = _dot_nt(qp, _block_diag(kp))
                for li in range(len(LEVELS)):
                    dl = decl_scr[par, li * C:(li + 1) * C, sl2]
                    s_l = _dot_nt(qp * dl, _block_diag(kp * dl))
                    sc = jnp.where(level_masks[li], s_l, sc)
                sc_scr[par, pr] = jnp.where(causal, sc, 0.0).astype(BF16)
        if gates:
            fl = proj_scr[chunk_rows(i + 1), _F0:_F0 + KDIM]
            e = jnp.exp(-jnp.abs(fl))
            r = 1.0 / (1.0 + e)
            er = e * r
            nonneg = fl >= 0.0
            sig = jnp.where(nonneg, r, er)
            sig_neg = jnp.where(nonneg, er, r)
            f = lb + oml * sig
            lf = jnp.log2(jnp.maximum(f, LOG_FLOOR))
            kk_scr[prv] = oml * sig_neg
            lf_hi = lf.astype(BF16)
            lf_lo = (lf - lf_hi.astype(F32)).astype(BF16)
            expo = _dot(wexp, jnp.concatenate([lf_hi, lf_lo], axis=0))
            b = expo[0:C, :]
            dec_scr[prv, 0:C, :] = jnp.exp2(b)
            dec_scr[prv, C:2 * C, :] = jnp.exp2(b[C - 1:C, :] - b)
            row = 0
            for c in LEVELS:
                if c in MXU_LEVELS:
                    k = 1 + MXU_LEVELS.index(c)
                    blk = expo[k * C:(k + 1) * C, :]
                else:
                    parts = []
                    for r0 in range(0, C, 2 * c):
                        mid = b[r0 + c - 1:r0 + c, :]
                        parts += [mid - b[r0:r0 + c, :], b[r0 + c:r0 + 2 * c, :] - mid]
                    blk = jnp.concatenate(parts, axis=0)
                decl_scr[prv, row * C:(row + 1) * C, :] = jnp.exp2(blk.astype(BF16))
                row += 1
        if finish:
            g = proj_scr[rows_f, _G0:_G0 + WIDTH]
            for hd, sl in enumerate(heads):
                o = o_pairs[hd // 2][:, (hd % 2) * HEAD_V:(hd % 2 + 1) * HEAD_V]
                o = o * lax.rsqrt(jnp.mean(o * o, axis=-1, keepdims=True) + EPS)
                a_scr[rows_f, sl] = o * ogain[:, sl] * _sigmoid(g[:, sl])

    step(-1, 1, finish=False, scores=False)
    for i in range(nch):
        step(i, i % 2, finish=i > 0, gates=i < nch - 1)
    step(nch, nch % 2, scores=False, gates=False)

    vp = late_scr[:, 0:POOL_WIDTH]
    pbuf_scr[POOL_HIST:POOL_HIST + T, :] = vp
    ys = []
    for gi, w in enumerate(POOL_WINDOWS):
        cs = slice(gi * POOL_GROUP, (gi + 1) * POOL_GROUP)
        s = pbuf_scr[:, cs]
        span = 1
        while span < w:
            s = s + pltpu.roll(s, span, 0)
            span *= 2
        s = s[POOL_HIST:]
        if meta:
            pos = lax.broadcasted_iota(jnp.int32, (T, 1), 0) - (T - N_META)
            cnt = jnp.clip(pos + 1, 1, w).astype(F32)
            pooled = s / cnt - vp[:, cs]
        else:
            pooled = s * (1.0 / w) - vp[:, cs]
        ys.append(_dot(pooled.astype(BF16), pproj_ref[gi]))
    p = jnp.concatenate(ys, axis=-1) * pscale_ref[...]
    hist = pbuf_scr[T:T + POOL_HIST, :]
    pbuf_scr[0:POOL_HIST, :] = hist

    br_a = _dot(a_scr[...].astype(BF16), wbh_ref[...])
    br_p = _dot(p.astype(BF16), wbp_ref[...])
    z = (_sigmoid(late_scr[:, _A0 - _P0:_B0 - _P0]) * br_a
         + _sigmoid(late_scr[:, _B0 - _P0:IN_COLS - _P0]) * br_p)
    zz = _dot(z.astype(BF16), wout_ref[...])
    o_ref[...] = h + _rms(zz, gpost_ref[...])

    if meta:
        st_out_ref[...] = state_scr[...]
        ph_out_ref[...] = hist


def _ffn_kernel(h_ref, c0_ref, gpre_ref, wg_ref, wu_ref, cw_ref, cb_ref, wd_ref, gpost_ref, *rest,
                tile, meta):
    if meta:
        o_ref, c_out_ref, hist_scr, act_scr = rest
    else:
        o_ref, hist_scr, act_scr = rest
    T = tile

    @pl.when(pl.program_id(1) == 0)
    def _():
        hist_scr[...] = c0_ref[...]

    h = h_ref[...]
    u = _rms(h, gpre_ref[...]).astype(BF16)
    row8 = lax.broadcasted_iota(jnp.int32, (SUBLANES, FF_BLOCK), 0)

    def shifted(g, prev, k):
        rolled = pltpu.roll(g, k, 0)
        top = jnp.where(row8 < k, pltpu.roll(prev, k, 0), rolled[0:SUBLANES])
        return jnp.concatenate([top, rolled[SUBLANES:]], axis=0)

    for j in range(D_FF // FF_BLOCK):
        cols = slice(j * FF_BLOCK, (j + 1) * FF_BLOCK)
        g = _dot(u, wg_ref[:, cols])
        up = _dot(u, wu_ref[:, cols])
        prev = hist_scr[:, cols]
        hist_scr[:, cols] = g[T - SUBLANES:T]
        cw = cw_ref[:, cols]
        gc = cb_ref[:, cols] + shifted(g, prev, 2) * cw[0:1] + shifted(g, prev, 1) * cw[1:2] + g * cw[2:3]
        e = jnp.exp2(gc * (_GELU_K1 + _GELU_K3 * (gc * gc)))
        act_scr[:, cols] = ((gc * up) * (1.0 / (1.0 + e))).astype(BF16)

    y = _dot(act_scr[...], wd_ref[...])
    o_ref[...] = h + _rms(y, gpost_ref[...])
    if meta:
        c_out_ref[...] = hist_scr[...]


class _Layer:
    def __init__(self, stacked, layer):
        self.array, self.layer = stacked, layer


def _const_spec(operand):
    if isinstance(operand, _Layer):
        shape, layer = operand.array.shape[1:], operand.layer
        return pl.BlockSpec((None,) + shape, lambda b, t: (layer,) + (0,) * len(shape),
                            pipeline_mode=pl.Buffered(1))
    nd = len(operand.shape)
    return pl.BlockSpec(operand.shape, lambda b, t: (0,) * nd, pipeline_mode=pl.Buffered(1))


def _arrays(operands):
    return [o.array if isinstance(o, _Layer) else o for o in operands]


def _mixer(h, st0, ph0, weights, wexp, *, tile, layer, meta):
    bsz, seq, _ = h.shape
    nt = seq // tile
    tok_spec = pl.BlockSpec((None, tile, D_MODEL), lambda b, t: (b, t, 0))
    consts = (st0, ph0) + tuple(weights) + (wexp,)
    out_shape = [jax.ShapeDtypeStruct(h.shape, F32)]
    out_specs = [tok_spec]
    if meta:
        out_shape += [jax.ShapeDtypeStruct(st0.shape, F32), jax.ShapeDtypeStruct(ph0.shape, F32)]
        out_specs += [_const_spec(st0), _const_spec(ph0)]
    out = pl.pallas_call(
        functools.partial(_mixer_kernel, tile=tile, layer=layer, meta=meta),
        grid=(bsz, nt),
        in_specs=[tok_spec] + [_const_spec(c) for c in consts],
        out_specs=out_specs,
        out_shape=out_shape,
        scratch_shapes=[
            pltpu.VMEM((tile, _P0), F32),
            pltpu.VMEM((tile, IN_COLS - _P0), F32),
            pltpu.VMEM((tile, D_MODEL), BF16),
            pltpu.VMEM((HEADS, HEAD_V, HEAD_K), F32),
            pltpu.VMEM((POOL_HIST + tile, POOL_WIDTH), F32),
            pltpu.VMEM((tile, WIDTH), F32),
            pltpu.VMEM((2, HEADS // 2, CHUNK, 2 * CHUNK), BF16),
            pltpu.VMEM((2, CHUNK, KDIM), BF16),
            pltpu.VMEM((2, CHUNK, KDIM), F32),
            pltpu.VMEM((2, 2 * CHUNK, KDIM), F32),
            pltpu.VMEM((2, len(LEVELS) * CHUNK, KDIM), BF16),
            pltpu.VMEM((2, HEADS, HEAD_V, HEAD_K), BF16),
        ],
        compiler_params=pltpu.CompilerParams(
            dimension_semantics=("arbitrary", "arbitrary"), vmem_limit_bytes=VMEM_LIMIT_BYTES),
        name="mixer_meta" if meta else "mixer",
    )(h, *_arrays(consts))
    return out if meta else out[0]


def _ffn(h, c0, weights, *, tile, meta):
    bsz, seq, _ = h.shape
    nt = seq // tile
    tok_spec = pl.BlockSpec((None, tile, D_MODEL), lambda b, t: (b, t, 0))
    consts = (c0,) + tuple(weights)
    out_shape = [jax.ShapeDtypeStruct(h.shape, F32)]
    out_specs = [tok_spec]
    if meta:
        out_shape += [jax.ShapeDtypeStruct(c0.shape, F32)]
        out_specs += [_const_spec(c0)]
    out = pl.pallas_call(
        functools.partial(_ffn_kernel, tile=tile, meta=meta),
        grid=(bsz, nt),
        in_specs=[tok_spec] + [_const_spec(c) for c in consts],
        out_specs=out_specs,
        out_shape=out_shape,
        scratch_shapes=[
            pltpu.VMEM((SUBLANES, D_FF), F32),
            pltpu.VMEM((tile, D_FF), BF16),
        ],
        compiler_params=pltpu.CompilerParams(
            dimension_semantics=("arbitrary", "arbitrary"), vmem_limit_bytes=VMEM_LIMIT_BYTES),
        name="ffn_meta" if meta else "ffn",
    )(h, *_arrays(consts))
    return out if meta else out[0]


def _pick_tile(seq, largest):
    t = largest
    while t >= CHUNK:
        if seq % t == 0:
            return t
        t //= 2
    raise ValueError(f"sequence length {seq} must be a multiple of {CHUNK}")


def kernel(x, meta_tokens, mix_norm_pre, mix_norm_post, w_in, hgrn_lower_bounds, hgrn_out_norm, w_branch_hgrn, pool_proj, pool_scale, w_branch_pool, w_out, ffn_norm_pre, ffn_norm_post, ffn_w_gate, ffn_w_up, ffn_conv_w, ffn_conv_b, ffn_w_down):
    bsz, seq, d = x.shape
    assert d == D_MODEL and meta_tokens.shape == (N_META, D_MODEL)
    mix_tile = _pick_tile(seq, MIXER_TILE)
    ffn_tile = _pick_tile(seq, FFN_TILE)
    wexp = jnp.asarray(_decay_sum_matrix(), BF16)
    row = lambda a: a.reshape(1, -1).astype(F32)

    h = x.astype(F32)
    hm = jnp.concatenate([jnp.zeros((CHUNK - N_META, D_MODEL), F32), meta_tokens.astype(F32)], axis=0)[None]
    st_zero = jnp.zeros((HEADS, HEAD_V, HEAD_K), F32)
    ph_zero = jnp.zeros((POOL_HIST, POOL_WIDTH), F32)
    cv_zero = jnp.zeros((SUBLANES, D_FF), F32)
    lbraw = hgrn_lower_bounds.astype(F32)

    bf = lambda a: a.astype(BF16)
    w_in_b, w_bh_b, pproj_b, w_bp_b, w_out_b = bf(w_in), bf(w_branch_hgrn), bf(pool_proj), bf(w_branch_pool), bf(w_out)
    w_gate_b, w_up_b, w_down_b = bf(ffn_w_gate), bf(ffn_w_up), bf(ffn_w_down)
    conv_w = ffn_conv_w.astype(F32)

    for l in range(DEPTH):
        mix_w = (row(mix_norm_pre[l]), _Layer(w_in_b, l), lbraw, row(hgrn_out_norm[l]),
                 _Layer(w_bh_b, l), _Layer(pproj_b, l), row(pool_scale[l]),
                 _Layer(w_bp_b, l), _Layer(w_out_b, l), row(mix_norm_post[l]))
        ffn_w = (row(ffn_norm_pre[l]), _Layer(w_gate_b, l), _Layer(w_up_b, l),
                 _Layer(conv_w, l), row(ffn_conv_b[l]), _Layer(w_down_b, l),
                 row(ffn_norm_post[l]))
        hm, st, ph = _mixer(hm, st_zero, ph_zero, mix_w, wexp, tile=CHUNK, layer=l, meta=True)
        h = _mixer(h, st, ph, mix_w, wexp, tile=mix_tile, layer=l, meta=False)
        hm, cv = _ffn(hm, cv_zero, ffn_w, tile=CHUNK, meta=True)
        h = _ffn(h, cv, ffn_w, tile=ffn_tile, meta=False)
    return h.astype(x.dtype)
```

```python
import functools
import math

import numpy as np
import jax
import jax.numpy as jnp
from jax import lax
from jax.experimental import pallas as pl
from jax.experimental.pallas import tpu as pltpu

D_MODEL = 1024
DEPTH = 2
N_META = 16
HEADS = 4
HEAD_K = 128
HEAD_V = 128
KDIM = HEADS * HEAD_K
WIDTH = HEADS * HEAD_V
CHUNK = 64
POOL_WINDOWS = (2, 4, 8, 16)
POOL_GROUP = 128
POOL_WIDTH = POOL_GROUP * len(POOL_WINDOWS)
POOL_HIST = 16
D_FF = 2816
FF_BLOCK = 256
PROJ_BLOCK = 256
CONV_WIDTH = 3
SUBLANES = 8
EPS = 1e-6
LOG_FLOOR = 1e-30
IN_COLS = 2 * KDIM + 2 * WIDTH + POOL_WIDTH + 2 * D_MODEL

_Q0, _F0, _I0, _G0, _P0, _A0, _B0 = (0, KDIM, 2 * KDIM, 2 * KDIM + WIDTH, 2 * KDIM + 2 * WIDTH,
                                     2 * KDIM + 2 * WIDTH + POOL_WIDTH,
                                     2 * KDIM + 2 * WIDTH + POOL_WIDTH + D_MODEL)

LEVELS = (32, 16, 8, 4, 2, 1)
VMEM_LIMIT_BYTES = 60 * 1024 * 1024
MIXER_TILE = 512
FFN_TILE = 1024

_GELU_K1 = -2.0 * math.sqrt(2.0 / math.pi) * math.log2(math.e)
_GELU_K3 = _GELU_K1 * 0.044715

F32 = jnp.float32
BF16 = jnp.bfloat16


def _decay_sum_matrix():
    c64 = CHUNK
    w = np.zeros(((2 + len(LEVELS)) * c64, c64), np.float32)
    for t in range(c64):
        w[t, :t + 1] = 1.0
        w[c64 + t, t + 1:] = 1.0
    for li, c in enumerate(LEVELS):
        base = (2 + li) * c64
        for r in range(c64):
            mid = r - r % (2 * c) + c - 1
            if r > mid:
                w[base + r, mid + 1:r + 1] = 1.0
            else:
                w[base + r, r + 1:mid + 1] = 1.0
    return np.concatenate([w, w], axis=1)


def _rms(x, gain):
    ms = jnp.mean(x * x, axis=-1, keepdims=True)
    return x * lax.rsqrt(ms + EPS) * gain


def _sigmoid(x):
    return 1.0 / (1.0 + jnp.exp(-x))


def _dot(a, b):
    return jnp.dot(a, b, preferred_element_type=F32)


def _lower_bound(lbraw, layer):
    rows = [lbraw[j:j + 1, :] for j in range(DEPTH)]
    m = functools.reduce(jnp.maximum, rows)
    es = [jnp.exp(r - m) for r in rows]
    tot = functools.reduce(lambda a, b: a + b, es)
    gam = [e / tot for e in es]
    cum = functools.reduce(lambda a, b: a + b, gam[:layer + 1])
    return jnp.clip(cum - gam[0], 0.0, 1.0)


def _mixer_kernel(h_ref, st0_ref, ph0_ref, gpre_ref, win_ref, lbraw_ref, ogain_ref, wbh_ref, pproj_ref,
                  pscale_ref, wbp_ref, wout_ref, gpost_ref, wexp_ref, *rest, tile, layer, meta):
    if meta:
        o_ref, st_out_ref, ph_out_ref, *scratch = rest
    else:
        o_ref, *scratch = rest
    proj_scr, late_scr, u_scr, state_scr, pbuf_scr, a_scr, o_scr, qb_scr, kk_scr, dec_scr, sprev_scr = scratch
    T = tile
    C = CHUNK

    @pl.when(pl.program_id(1) == 0)
    def _():
        state_scr[...] = st0_ref[...]
        pbuf_scr[0:POOL_HIST, :] = ph0_ref[...]

    h = h_ref[...]
    u_scr[...] = _rms(h, gpre_ref[...]).astype(BF16)
    nch = T // C
    proj_scr[...] = _dot(u_scr[...], win_ref[:, 0:_P0])
    late_blocks = (IN_COLS - _P0) // PROJ_BLOCK
    per_step = -(-late_blocks // (nch + 2))

    def project_late(i):
        first = (i + 1) * per_step
        for b in range(max(0, min(per_step, late_blocks - first))):
            start = (first + b) * PROJ_BLOCK
            late_scr[:, start:start + PROJ_BLOCK] = _dot(
                u_scr[...], win_ref[:, _P0 + start:_P0 + start + PROJ_BLOCK])

    lb = _lower_bound(lbraw_ref[...], layer)
    oml = 1.0 - lb
    ogain = ogain_ref[...]
    wexp = wexp_ref[...]
    ti = lax.broadcasted_iota(jnp.int32, (C, C), 0)
    si = lax.broadcasted_iota(jnp.int32, (C, C), 1)
    xor = ti ^ si
    level_masks = [(xor >= c) & (xor < 2 * c) for c in LEVELS]
    causal = ti >= si

    def chunk_rows(ci):
        return slice(ci * C, (ci + 1) * C)

    heads = [slice(hd * HEAD_K, (hd + 1) * HEAD_K) for hd in range(HEADS)]

    def step(i, par, finish=True, scores=True, gates=True):
        prv = 1 - par
        if finish:
            o_inter = [_dot(qb_scr[prv, :, sl], sprev_scr[prv, hd]) for hd, sl in enumerate(heads)]
        project_late(i)
        if scores:
            rows = chunk_rows(i)
            dec = dec_scr.at[par]
            q = proj_scr[rows, _Q0:_Q0 + KDIM]
            v = proj_scr[rows, _I0:_I0 + WIDTH]
            kk = kk_scr[par]
            for hd, sl in enumerate(heads):
                kdec = (kk[:, sl] * dec[C:2 * C, sl]).astype(BF16)
                st_t = state_scr[hd]
                sprev_scr[par, hd] = st_t.T.astype(BF16)
                state_scr[hd] = st_t * dec[C - 1:C, sl] + _dot(v[:, sl].T.astype(BF16), kdec)
            qb_scr[par] = (q * dec[0:C, :]).astype(BF16)
            scs = []
            for hd, sl in enumerate(heads):
                qh, kh = q[:, sl], kk[:, sl]
                sc = _dot(qh.astype(BF16), kh.T.astype(BF16))
                for li in range(len(LEVELS)):
                    dl = dec[(2 + li) * C:(3 + li) * C, sl]
                    s_l = _dot((qh * dl).astype(BF16), (kh * dl).T.astype(BF16))
                    sc = jnp.where(level_masks[li], s_l, sc)
                scs.append(jnp.where(causal, sc, 0.0).astype(BF16))
        if gates:
            fl = proj_scr[chunk_rows(i + 1), _F0:_F0 + KDIM]
            e = jnp.exp(-jnp.abs(fl))
            r = 1.0 / (1.0 + e)
            er = e * r
            nonneg = fl >= 0.0
            sig = jnp.where(nonneg, r, er)
            sig_neg = jnp.where(nonneg, er, r)
            f = lb + oml * sig
            lf = jnp.log2(jnp.maximum(f, LOG_FLOOR))
            kk_scr[prv] = oml * sig_neg
            lf_hi = lf.astype(BF16)
            lf_lo = (lf - lf_hi.astype(F32)).astype(BF16)
            expo = _dot(wexp, jnp.concatenate([lf_hi, lf_lo], axis=0))
            dec_scr[prv] = jnp.exp2(expo)
        if finish:
            rows_f = chunk_rows(i - 1)
            g = proj_scr[rows_f, _G0:_G0 + WIDTH]
            for hd, sl in enumerate(heads):
                o = o_scr[prv, :, sl] + o_inter[hd]
                o = o * lax.rsqrt(jnp.mean(o * o, axis=-1, keepdims=True) + EPS)
                a_scr[rows_f, sl] = o * ogain[:, sl] * _sigmoid(g[:, sl])
        if scores:
            for hd, sl in enumerate(heads):
                o_scr[par, :, sl] = _dot(scs[hd], v[:, sl].astype(BF16))

    step(-1, 1, finish=False, scores=False)
    for i in range(nch):
        step(i, i % 2, finish=i > 0, gates=i < nch - 1)
    step(nch, nch % 2, scores=False, gates=False)

    vp = late_scr[:, 0:POOL_WIDTH]
    pbuf_scr[POOL_HIST:POOL_HIST + T, :] = vp
    ys = []
    for gi, w in enumerate(POOL_WINDOWS):
        cs = slice(gi * POOL_GROUP, (gi + 1) * POOL_GROUP)
        s = pbuf_scr[:, cs]
        span = 1
        while span < w:
            s = s + pltpu.roll(s, span, 0)
            span *= 2
        s = s[POOL_HIST:]
        if meta:
            pos = lax.broadcasted_iota(jnp.int32, (T, 1), 0) - (T - N_META)
            cnt = jnp.clip(pos + 1, 1, w).astype(F32)
            pooled = s / cnt - vp[:, cs]
        else:
            pooled = s * (1.0 / w) - vp[:, cs]
        ys.append(_dot(pooled.astype(BF16), pproj_ref[gi]))
    p = jnp.concatenate(ys, axis=-1) * pscale_ref[...]
    hist = pbuf_scr[T:T + POOL_HIST, :]
    pbuf_scr[0:POOL_HIST, :] = hist

    br_a = _dot(a_scr[...].astype(BF16), wbh_ref[...])
    br_p = _dot(p.astype(BF16), wbp_ref[...])
    z = (_sigmoid(late_scr[:, _A0 - _P0:_B0 - _P0]) * br_a
         + _sigmoid(late_scr[:, _B0 - _P0:IN_COLS - _P0]) * br_p)
    zz = _dot(z.astype(BF16), wout_ref[...])
    o_ref[...] = h + _rms(zz, gpost_ref[...])

    if meta:
        st_out_ref[...] = state_scr[...]
        ph_out_ref[...] = hist


def _ffn_kernel(h_ref, c0_ref, gpre_ref, wg_ref, wu_ref, cw_ref, cb_ref, wd_ref, gpost_ref, *rest,
                tile, meta):
    if meta:
        o_ref, c_out_ref, hist_scr, act_scr = rest
    else:
        o_ref, hist_scr, act_scr = rest
    T = tile

    @pl.when(pl.program_id(1) == 0)
    def _():
        hist_scr[...] = c0_ref[...]

    h = h_ref[...]
    u = _rms(h, gpre_ref[...]).astype(BF16)
    row8 = lax.broadcasted_iota(jnp.int32, (SUBLANES, FF_BLOCK), 0)

    def shifted(g, prev, k):
        rolled = pltpu.roll(g, k, 0)
        top = jnp.where(row8 < k, pltpu.roll(prev, k, 0), rolled[0:SUBLANES])
        return jnp.concatenate([top, rolled[SUBLANES:]], axis=0)

    for j in range(D_FF // FF_BLOCK):
        cols = slice(j * FF_BLOCK, (j + 1) * FF_BLOCK)
        g = _dot(u, wg_ref[:, cols])
        up = _dot(u, wu_ref[:, cols])
        prev = hist_scr[:, cols]
        hist_scr[:, cols] = g[T - SUBLANES:T]
        cw = cw_ref[:, cols]
        gc = cb_ref[:, cols] + shifted(g, prev, 2) * cw[0:1] + shifted(g, prev, 1) * cw[1:2] + g * cw[2:3]
        e = jnp.exp2(gc * (_GELU_K1 + _GELU_K3 * (gc * gc)))
        act_scr[:, cols] = ((gc * up) * (1.0 / (1.0 + e))).astype(BF16)

    y = _dot(act_scr[...], wd_ref[...])
    o_ref[...] = h + _rms(y, gpost_ref[...])
    if meta:
        c_out_ref[...] = hist_scr[...]


class _Layer:
    def __init__(self, stacked, layer):
        self.array, self.layer = stacked, layer


def _const_spec(operand):
    if isinstance(operand, _Layer):
        shape, layer = operand.array.shape[1:], operand.layer
        return pl.BlockSpec((None,) + shape, lambda b, t: (layer,) + (0,) * len(shape),
                            pipeline_mode=pl.Buffered(1))
    nd = len(operand.shape)
    return pl.BlockSpec(operand.shape, lambda b, t: (0,) * nd, pipeline_mode=pl.Buffered(1))


def _arrays(operands):
    return [o.array if isinstance(o, _Layer) else o for o in operands]


def _mixer(h, st0, ph0, weights, wexp, *, tile, layer, meta):
    bsz, seq, _ = h.shape
    nt = seq // tile
    tok_spec = pl.BlockSpec((None, tile, D_MODEL), lambda b, t: (b, t, 0))
    consts = (st0, ph0) + tuple(weights) + (wexp,)
    out_shape = [jax.ShapeDtypeStruct(h.shape, F32)]
    out_specs = [tok_spec]
    if meta:
        out_shape += [jax.ShapeDtypeStruct(st0.shape, F32), jax.ShapeDtypeStruct(ph0.shape, F32)]
        out_specs += [_const_spec(st0), _const_spec(ph0)]
    out = pl.pallas_call(
        functools.partial(_mixer_kernel, tile=tile, layer=layer, meta=meta),
        grid=(bsz, nt),
        in_specs=[tok_spec] + [_const_spec(c) for c in consts],
        out_specs=out_specs,
        out_shape=out_shape,
        scratch_shapes=[
            pltpu.VMEM((tile, _P0), F32),
            pltpu.VMEM((tile, IN_COLS - _P0), F32),
            pltpu.VMEM((tile, D_MODEL), BF16),
            pltpu.VMEM((HEADS, HEAD_V, HEAD_K), F32),
            pltpu.VMEM((POOL_HIST + tile, POOL_WIDTH), F32),
            pltpu.VMEM((tile, WIDTH), F32),
            pltpu.VMEM((2, CHUNK, WIDTH), F32),
            pltpu.VMEM((2, CHUNK, KDIM), BF16),
            pltpu.VMEM((2, CHUNK, KDIM), F32),
            pltpu.VMEM((2, (2 + len(LEVELS)) * CHUNK, KDIM), F32),
            pltpu.VMEM((2, HEADS, HEAD_K, HEAD_V), BF16),
        ],
        compiler_params=pltpu.CompilerParams(
            dimension_semantics=("arbitrary", "arbitrary"), vmem_limit_bytes=VMEM_LIMIT_BYTES),
        name="mixer_meta" if meta else "mixer",
    )(h, *_arrays(consts))
    return out if meta else out[0]


def _ffn(h, c0, weights, *, tile, meta):
    bsz, seq, _ = h.shape
    nt = seq // tile
    tok_spec = pl.BlockSpec((None, tile, D_MODEL), lambda b, t: (b, t, 0))
    consts = (c0,) + tuple(weights)
    out_shape = [jax.ShapeDtypeStruct(h.shape, F32)]
    out_specs = [tok_spec]
    if meta:
        out_shape += [jax.ShapeDtypeStruct(c0.shape, F32)]
        out_specs += [_const_spec(c0)]
    out = pl.pallas_call(
        functools.partial(_ffn_kernel, tile=tile, meta=meta),
        grid=(bsz, nt),
        in_specs=[tok_spec] + [_const_spec(c) for c in consts],
        out_specs=out_specs,
        out_shape=out_shape,
        scratch_shapes=[
            pltpu.VMEM((SUBLANES, D_FF), F32),
            pltpu.VMEM((tile, D_FF), BF16),
        ],
        compiler_params=pltpu.CompilerParams(
            dimension_semantics=("arbitrary", "arbitrary"), vmem_limit_bytes=VMEM_LIMIT_BYTES),
        name="ffn_meta" if meta else "ffn",
    )(h, *_arrays(consts))
    return out if meta else out[0]


def _pick_tile(seq, largest):
    t = largest
    while t >= CHUNK:
        if seq % t == 0:
            return t
        t //= 2
    raise ValueError(f"sequence length {seq} must be a multiple of {CHUNK}")


def kernel(x, meta_tokens, mix_norm_pre, mix_norm_post, w_in, hgrn_lower_bounds, hgrn_out_norm, w_branch_hgrn, pool_proj, pool_scale, w_branch_pool, w_out, ffn_norm_pre, ffn_norm_post, ffn_w_gate, ffn_w_up, ffn_conv_w, ffn_conv_b, ffn_w_down):
    bsz, seq, d = x.shape
    assert d == D_MODEL and meta_tokens.shape == (N_META, D_MODEL)
    mix_tile = _pick_tile(seq, MIXER_TILE)
    ffn_tile = _pick_tile(seq, FFN_TILE)
    wexp = jnp.asarray(_decay_sum_matrix(), BF16)
    row = lambda a: a.reshape(1, -1).astype(F32)

    h = x.astype(F32)
    hm = jnp.concatenate([jnp.zeros((CHUNK - N_META, D_MODEL), F32), meta_tokens.astype(F32)], axis=0)[None]
    st_zero = jnp.zeros((HEADS, HEAD_V, HEAD_K), F32)
    ph_zero = jnp.zeros((POOL_HIST, POOL_WIDTH), F32)
    cv_zero = jnp.zeros((SUBLANES, D_FF), F32)
    lbraw = hgrn_lower_bounds.astype(F32)

    bf = lambda a: a.astype(BF16)
    w_in_b, w_bh_b, pproj_b, w_bp_b, w_out_b = bf(w_in), bf(w_branch_hgrn), bf(pool_proj), bf(w_branch_pool), bf(w_out)
    w_gate_b, w_up_b, w_down_b = bf(ffn_w_gate), bf(ffn_w_up), bf(ffn_w_down)
    conv_w = ffn_conv_w.astype(F32)

    for l in range(DEPTH):
        mix_w = (row(mix_norm_pre[l]), _Layer(w_in_b, l), lbraw, row(hgrn_out_norm[l]),
                 _Layer(w_bh_b, l), _Layer(pproj_b, l), row(pool_scale[l]),
                 _Layer(w_bp_b, l), _Layer(w_out_b, l), row(mix_norm_post[l]))
        ffn_w = (row(ffn_norm_pre[l]), _Layer(w_gate_b, l), _Layer(w_up_b, l),
                 _Layer(conv_w, l), row(ffn_conv_b[l]), _Layer(w_down_b, l),
                 row(ffn_norm_post[l]))
        hm, st, ph = _mixer(hm, st_zero, ph_zero, mix_w, wexp, tile=CHUNK, layer=l, meta=True)
        h = _mixer(h, st, ph, mix_w, wexp, tile=mix_tile, layer=l, meta=False)
        hm, cv = _ffn(hm, cv_zero, ffn_w, tile=CHUNK, meta=True)
        h = _ffn(h, cv, ffn_w, tile=ffn_tile, meta=False)
    return h.astype(x.dtype)
```

```python
import functools
import math

import numpy as np
import jax
import jax.numpy as jnp
from jax import lax
from jax.experimental import pallas as pl
from jax.experimental.pallas import tpu as pltpu

D_MODEL = 1024
DEPTH = 2
N_META = 16
HEADS = 4
HEAD_K = 128
HEAD_V = 128
KDIM = HEADS * HEAD_K
WIDTH = HEADS * HEAD_V
CHUNK = 64
POOL_WINDOWS = (2, 4, 8, 16)
POOL_GROUP = 128
POOL_WIDTH = POOL_GROUP * len(POOL_WINDOWS)
POOL_HIST = 16
D_FF = 2816
FF_BLOCK = 256
PROJ_BLOCK = 256
CONV_WIDTH = 3
SUBLANES = 8
EPS = 1e-6
LOG_FLOOR = 1e-30
IN_COLS = 2 * KDIM + 2 * WIDTH + POOL_WIDTH + 2 * D_MODEL

_Q0, _F0, _I0, _G0, _P0, _A0, _B0 = (0, KDIM, 2 * KDIM, 2 * KDIM + WIDTH, 2 * KDIM + 2 * WIDTH,
                                     2 * KDIM + 2 * WIDTH + POOL_WIDTH,
                                     2 * KDIM + 2 * WIDTH + POOL_WIDTH + D_MODEL)

LEVELS = (32, 16, 8, 4, 2, 1)
MXU_LEVELS = tuple(c for c in LEVELS if c < SUBLANES)
VMEM_LIMIT_BYTES = 60 * 1024 * 1024
MIXER_TILE = 512
FFN_TILE = 1024

_GELU_K1 = -2.0 * math.sqrt(2.0 / math.pi) * math.log2(math.e)
_GELU_K3 = _GELU_K1 * 0.044715

F32 = jnp.float32
BF16 = jnp.bfloat16


def _decay_sum_matrix():
    c64 = CHUNK
    w = np.zeros(((1 + len(MXU_LEVELS)) * c64, c64), np.float32)
    for t in range(c64):
        w[t, :t + 1] = 1.0
    for li, c in enumerate(MXU_LEVELS):
        base = (1 + li) * c64
        for r in range(c64):
            mid = r - r % (2 * c) + c - 1
            if r > mid:
                w[base + r, mid + 1:r + 1] = 1.0
            else:
                w[base + r, r + 1:mid + 1] = 1.0
    return np.concatenate([w, w], axis=1)


def _rms(x, gain):
    ms = jnp.mean(x * x, axis=-1, keepdims=True)
    return x * lax.rsqrt(ms + EPS) * gain


def _sigmoid(x):
    return 1.0 / (1.0 + jnp.exp(-x))


def _dot(a, b):
    return jnp.dot(a, b, preferred_element_type=F32)


def _lower_bound(lbraw, layer):
    rows = [lbraw[j:j + 1, :] for j in range(DEPTH)]
    m = functools.reduce(jnp.maximum, rows)
    es = [jnp.exp(r - m) for r in rows]
    tot = functools.reduce(lambda a, b: a + b, es)
    gam = [e / tot for e in es]
    cum = functools.reduce(lambda a, b: a + b, gam[:layer + 1])
    return jnp.clip(cum - gam[0], 0.0, 1.0)


def _mixer_kernel(h_ref, st0_ref, ph0_ref, gpre_ref, win_ref, lbraw_ref, ogain_ref, wbh_ref, pproj_ref,
                  pscale_ref, wbp_ref, wout_ref, gpost_ref, wexp_ref, *rest, tile, layer, meta):
    if meta:
        o_ref, st_out_ref, ph_out_ref, *scratch = rest
    else:
        o_ref, *scratch = rest
    proj_scr, late_scr, u_scr, state_scr, pbuf_scr, a_scr, o_scr, qb_scr, kk_scr, dec_scr, sprev_scr = scratch
    T = tile
    C = CHUNK

    @pl.when(pl.program_id(1) == 0)
    def _():
        state_scr[...] = st0_ref[...]
        pbuf_scr[0:POOL_HIST, :] = ph0_ref[...]

    h = h_ref[...]
    u_scr[...] = _rms(h, gpre_ref[...]).astype(BF16)
    nch = T // C
    proj_scr[...] = _dot(u_scr[...], win_ref[:, 0:_P0])
    late_blocks = (IN_COLS - _P0) // PROJ_BLOCK
    per_step = -(-late_blocks // (nch + 2))

    def project_late(i):
        first = (i + 1) * per_step
        for b in range(max(0, min(per_step, late_blocks - first))):
            start = (first + b) * PROJ_BLOCK
            late_scr[:, start:start + PROJ_BLOCK] = _dot(
                u_scr[...], win_ref[:, _P0 + start:_P0 + start + PROJ_BLOCK])

    lb = _lower_bound(lbraw_ref[...], layer)
    oml = 1.0 - lb
    ogain = ogain_ref[...]
    wexp = wexp_ref[...]
    ti = lax.broadcasted_iota(jnp.int32, (C, C), 0)
    si = lax.broadcasted_iota(jnp.int32, (C, C), 1)
    xor = ti ^ si
    level_masks = [(xor >= c) & (xor < 2 * c) for c in LEVELS]
    causal = ti >= si

    def chunk_rows(ci):
        return slice(ci * C, (ci + 1) * C)

    heads = [slice(hd * HEAD_K, (hd + 1) * HEAD_K) for hd in range(HEADS)]

    def step(i, par, finish=True, scores=True, gates=True):
        prv = 1 - par
        if finish:
            o_inter = [_dot(qb_scr[prv, :, sl], sprev_scr[prv, hd]) for hd, sl in enumerate(heads)]
        project_late(i)
        if scores:
            rows = chunk_rows(i)
            dec = dec_scr.at[par]
            q = proj_scr[rows, _Q0:_Q0 + KDIM]
            v = proj_scr[rows, _I0:_I0 + WIDTH]
            kk = kk_scr[par]
            for hd, sl in enumerate(heads):
                kdec = (kk[:, sl] * dec[C:2 * C, sl]).astype(BF16)
                st_t = state_scr[hd]
                sprev_scr[par, hd] = st_t.T.astype(BF16)
                state_scr[hd] = st_t * dec[C - 1:C, sl] + _dot(v[:, sl].T.astype(BF16), kdec)
            qb_scr[par] = (q * dec[0:C, :]).astype(BF16)
            scs = []
            for hd, sl in enumerate(heads):
                qh, kh = q[:, sl], kk[:, sl]
                sc = _dot(qh.astype(BF16), kh.T.astype(BF16))
                for li in range(len(LEVELS)):
                    dl = dec[(2 + li) * C:(3 + li) * C, sl]
                    s_l = _dot((qh * dl).astype(BF16), (kh * dl).T.astype(BF16))
                    sc = jnp.where(level_masks[li], s_l, sc)
                scs.append(jnp.where(causal, sc, 0.0).astype(BF16))
        if gates:
            fl = proj_scr[chunk_rows(i + 1), _F0:_F0 + KDIM]
            e = jnp.exp(-jnp.abs(fl))
            r = 1.0 / (1.0 + e)
            er = e * r
            nonneg = fl >= 0.0
            sig = jnp.where(nonneg, r, er)
            sig_neg = jnp.where(nonneg, er, r)
            f = lb + oml * sig
            lf = jnp.log2(jnp.maximum(f, LOG_FLOOR))
            kk_scr[prv] = oml * sig_neg
            lf_hi = lf.astype(BF16)
            lf_lo = (lf - lf_hi.astype(F32)).astype(BF16)
            expo = _dot(wexp, jnp.concatenate([lf_hi, lf_lo], axis=0))
            b = expo[0:C, :]
            dec_scr[prv, 0:C, :] = jnp.exp2(b)
            dec_scr[prv, C:2 * C, :] = jnp.exp2(b[C - 1:C, :] - b)
            row = 2
            for c in LEVELS:
                if c in MXU_LEVELS:
                    k = 1 + MXU_LEVELS.index(c)
                    blk = expo[k * C:(k + 1) * C, :]
                else:
                    parts = []
                    for r0 in range(0, C, 2 * c):
                        mid = b[r0 + c - 1:r0 + c, :]
                        parts += [mid - b[r0:r0 + c, :], b[r0 + c:r0 + 2 * c, :] - mid]
                    blk = jnp.concatenate(parts, axis=0)
                dec_scr[prv, row * C:(row + 1) * C, :] = jnp.exp2(blk)
                row += 1
        if finish:
            rows_f = chunk_rows(i - 1)
            g = proj_scr[rows_f, _G0:_G0 + WIDTH]
            for hd, sl in enumerate(heads):
                o = o_scr[prv, :, sl] + o_inter[hd]
                o = o * lax.rsqrt(jnp.mean(o * o, axis=-1, keepdims=True) + EPS)
                a_scr[rows_f, sl] = o * ogain[:, sl] * _sigmoid(g[:, sl])
        if scores:
            for hd, sl in enumerate(heads):
                o_scr[par, :, sl] = _dot(scs[hd], v[:, sl].astype(BF16))

    step(-1, 1, finish=False, scores=False)
    for i in range(nch):
        step(i, i % 2, finish=i > 0, gates=i < nch - 1)
    step(nch, nch % 2, scores=False, gates=False)

    vp = late_scr[:, 0:POOL_WIDTH]
    pbuf_scr[POOL_HIST:POOL_HIST + T, :] = vp
    ys = []
    for gi, w in enumerate(POOL_WINDOWS):
        cs = slice(gi * POOL_GROUP, (gi + 1) * POOL_GROUP)
        s = pbuf_scr[:, cs]
        span = 1
        while span < w:
            s = s + pltpu.roll(s, span, 0)
            span *= 2
        s = s[POOL_HIST:]
        if meta:
            pos = lax.broadcasted_iota(jnp.int32, (T, 1), 0) - (T - N_META)
            cnt = jnp.clip(pos + 1, 1, w).astype(F32)
            pooled = s / cnt - vp[:, cs]
        else:
            pooled = s * (1.0 / w) - vp[:, cs]
        ys.append(_dot(pooled.astype(BF16), pproj_ref[gi]))
    p = jnp.concatenate(ys, axis=-1) * pscale_ref[...]
    hist = pbuf_scr[T:T + POOL_HIST, :]
    pbuf_scr[0:POOL_HIST, :] = hist

    br_a = _dot(a_scr[...].astype(BF16), wbh_ref[...])
    br_p = _dot(p.astype(BF16), wbp_ref[...])
    z = (_sigmoid(late_scr[:, _A0 - _P0:_B0 - _P0]) * br_a
         + _sigmoid(late_scr[:, _B0 - _P0:IN_COLS - _P0]) * br_p)
    zz = _dot(z.astype(BF16), wout_ref[...])
    o_ref[...] = h + _rms(zz, gpost_ref[...])

    if meta:
        st_out_ref[...] = state_scr[...]
        ph_out_ref[...] = hist


def _ffn_kernel(h_ref, c0_ref, gpre_ref, wg_ref, wu_ref, cw_ref, cb_ref, wd_ref, gpost_ref, *rest,
                tile, meta):
    if meta:
        o_ref, c_out_ref, hist_scr, act_scr = rest
    else:
        o_ref, hist_scr, act_scr = rest
    T = tile

    @pl.when(pl.program_id(1) == 0)
    def _():
        hist_scr[...] = c0_ref[...]

    h = h_ref[...]
    u = _rms(h, gpre_ref[...]).astype(BF16)
    row8 = lax.broadcasted_iota(jnp.int32, (SUBLANES, FF_BLOCK), 0)

    def shifted(g, prev, k):
        rolled = pltpu.roll(g, k, 0)
        top = jnp.where(row8 < k, pltpu.roll(prev, k, 0), rolled[0:SUBLANES])
        return jnp.concatenate([top, rolled[SUBLANES:]], axis=0)

    for j in range(D_FF // FF_BLOCK):
        cols = slice(j * FF_BLOCK, (j + 1) * FF_BLOCK)
        g = _dot(u, wg_ref[:, cols])
        up = _dot(u, wu_ref[:, cols])
        prev = hist_scr[:, cols]
        hist_scr[:, cols] = g[T - SUBLANES:T]
        cw = cw_ref[:, cols]
        gc = cb_ref[:, cols] + shifted(g, prev, 2) * cw[0:1] + shifted(g, prev, 1) * cw[1:2] + g * cw[2:3]
        e = jnp.exp2(gc * (_GELU_K1 + _GELU_K3 * (gc * gc)))
        act_scr[:, cols] = ((gc * up) * (1.0 / (1.0 + e))).astype(BF16)

    y = _dot(act_scr[...], wd_ref[...])
    o_ref[...] = h + _rms(y, gpost_ref[...])
    if meta:
        c_out_ref[...] = hist_scr[...]


class _Layer:
    def __init__(self, stacked, layer):
        self.array, self.layer = stacked, layer


def _const_spec(operand):
    if isinstance(operand, _Layer):
        shape, layer = operand.array.shape[1:], operand.layer
        return pl.BlockSpec((None,) + shape, lambda b, t: (layer,) + (0,) * len(shape),
                            pipeline_mode=pl.Buffered(1))
    nd = len(operand.shape)
    return pl.BlockSpec(operand.shape, lambda b, t: (0,) * nd, pipeline_mode=pl.Buffered(1))


def _arrays(operands):
    return [o.array if isinstance(o, _Layer) else o for o in operands]


def _mixer(h, st0, ph0, weights, wexp, *, tile, layer, meta):
    bsz, seq, _ = h.shape
    nt = seq // tile
    tok_spec = pl.BlockSpec((None, tile, D_MODEL), lambda b, t: (b, t, 0))
    consts = (st0, ph0) + tuple(weights) + (wexp,)
    out_shape = [jax.ShapeDtypeStruct(h.shape, F32)]
    out_specs = [tok_spec]
    if meta:
        out_shape += [jax.ShapeDtypeStruct(st0.shape, F32), jax.ShapeDtypeStruct(ph0.shape, F32)]
        out_specs += [_const_spec(st0), _const_spec(ph0)]
    out = pl.pallas_call(
        functools.partial(_mixer_kernel, tile=tile, layer=layer, meta=meta),
        grid=(bsz, nt),
        in_specs=[tok_spec] + [_const_spec(c) for c in consts],
        out_specs=out_specs,
        out_shape=out_shape,
        scratch_shapes=[
            pltpu.VMEM((tile, _P0), F32),
            pltpu.VMEM((tile, IN_COLS - _P0), F32),
            pltpu.VMEM((tile, D_MODEL), BF16),
            pltpu.VMEM((HEADS, HEAD_V, HEAD_K), F32),
            pltpu.VMEM((POOL_HIST + tile, POOL_WIDTH), F32),
            pltpu.VMEM((tile, WIDTH), F32),
            pltpu.VMEM((2, CHUNK, WIDTH), F32),
            pltpu.VMEM((2, CHUNK, KDIM), BF16),
            pltpu.VMEM((2, CHUNK, KDIM), F32),
            pltpu.VMEM((2, (2 + len(LEVELS)) * CHUNK, KDIM), F32),
            pltpu.VMEM((2, HEADS, HEAD_K, HEAD_V), BF16),
        ],
        compiler_params=pltpu.CompilerParams(
            dimension_semantics=("arbitrary", "arbitrary"), vmem_limit_bytes=VMEM_LIMIT_BYTES),
        name="mixer_meta" if meta else "mixer",
    )(h, *_arrays(consts))
    return out if meta else out[0]


def _ffn(h, c0, weights, *, tile, meta):
    bsz, seq, _ = h.shape
    nt = seq // tile
    tok_spec = pl.BlockSpec((None, tile, D_MODEL), lambda b, t: (b, t, 0))
    consts = (c0,) + tuple(weights)
    out_shape = [jax.ShapeDtypeStruct(h.shape, F32)]
    out_specs = [tok_spec]
    if meta:
        out_shape += [jax.ShapeDtypeStruct(c0.shape, F32)]
        out_specs += [_const_spec(c0)]
    out = pl.pallas_call(
        functools.partial(_ffn_kernel, tile=tile, meta=meta),
        grid=(bsz, nt),
        in_specs=[tok_spec] + [_const_spec(c) for c in consts],
        out_specs=out_specs,
        out_shape=out_shape,
        scratch_shapes=[
            pltpu.VMEM((SUBLANES, D_FF), F32),
            pltpu.VMEM((tile, D_FF), BF16),
        ],
        compiler_params=pltpu.CompilerParams(
            dimension_semantics=("arbitrary", "arbitrary"), vmem_limit_bytes=VMEM_LIMIT_BYTES),
        name="ffn_meta" if meta else "ffn",
    )(h, *_arrays(consts))
    return out if meta else out[0]


def _pick_tile(seq, largest):
    t = largest
    while t >= CHUNK:
        if seq % t == 0:
            return t
        t //= 2
    raise ValueError(f"sequence length {seq} must be a multiple of {CHUNK}")


def kernel(x, meta_tokens, mix_norm_pre, mix_norm_post, w_in, hgrn_lower_bounds, hgrn_out_norm, w_branch_hgrn, pool_proj, pool_scale, w_branch_pool, w_out, ffn_norm_pre, ffn_norm_post, ffn_w_gate, ffn_w_up, ffn_conv_w, ffn_conv_b, ffn_w_down):
    bsz, seq, d = x.shape
    assert d == D_MODEL and meta_tokens.shape == (N_META, D_MODEL)
    mix_tile = _pick_tile(seq, MIXER_TILE)
    ffn_tile = _pick_tile(seq, FFN_TILE)
    wexp = jnp.asarray(_decay_sum_matrix(), BF16)
    row = lambda a: a.reshape(1, -1).astype(F32)

    h = x.astype(F32)
    hm = jnp.concatenate([jnp.zeros((CHUNK - N_META, D_MODEL), F32), meta_tokens.astype(F32)], axis=0)[None]
    st_zero = jnp.zeros((HEADS, HEAD_V, HEAD_K), F32)
    ph_zero = jnp.zeros((POOL_HIST, POOL_WIDTH), F32)
    cv_zero = jnp.zeros((SUBLANES, D_FF), F32)
    lbraw = hgrn_lower_bounds.astype(F32)

    bf = lambda a: a.astype(BF16)
    w_in_b, w_bh_b, pproj_b, w_bp_b, w_out_b = bf(w_in), bf(w_branch_hgrn), bf(pool_proj), bf(w_branch_pool), bf(w_out)
    w_gate_b, w_up_b, w_down_b = bf(ffn_w_gate), bf(ffn_w_up), bf(ffn_w_down)
    conv_w = ffn_conv_w.astype(F32)

    for l in range(DEPTH):
        mix_w = (row(mix_norm_pre[l]), _Layer(w_in_b, l), lbraw, row(hgrn_out_norm[l]),
                 _Layer(w_bh_b, l), _Layer(pproj_b, l), row(pool_scale[l]),
                 _Layer(w_bp_b, l), _Layer(w_out_b, l), row(mix_norm_post[l]))
        ffn_w = (row(ffn_norm_pre[l]), _Layer(w_gate_b, l), _Layer(w_up_b, l),
                 _Layer(conv_w, l), row(ffn_conv_b[l]), _Layer(w_down_b, l),
                 row(ffn_norm_post[l]))
        hm, st, ph = _mixer(hm, st_zero, ph_zero, mix_w, wexp, tile=CHUNK, layer=l, meta=True)
        h = _mixer(h, st, ph, mix_w, wexp, tile=mix_tile, layer=l, meta=False)
        hm, cv = _ffn(hm, cv_zero, ffn_w, tile=CHUNK, meta=True)
        h = _ffn(h, cv, ffn_w, tile=ffn_tile, meta=False)
    return h.astype(x.dtype)
```

```python
import functools
import math

import numpy as np
import jax
import jax.numpy as jnp
from jax import lax
from jax.experimental import pallas as pl
from jax.experimental.pallas import tpu as pltpu

D_MODEL = 1024
DEPTH = 2
N_META = 16
HEADS = 4
HEAD_K = 128
HEAD_V = 128
KDIM = HEADS * HEAD_K
WIDTH = HEADS * HEAD_V
CHUNK = 64
POOL_WINDOWS = (2, 4, 8, 16)
POOL_GROUP = 128
POOL_WIDTH = POOL_GROUP * len(POOL_WINDOWS)
POOL_HIST = 16
D_FF = 2816
FF_BLOCK = 256
PROJ_BLOCK = 256
CONV_WIDTH = 3
SUBLANES = 8
EPS = 1e-6
LOG_FLOOR = 1e-30
IN_COLS = 2 * KDIM + 2 * WIDTH + POOL_WIDTH + 2 * D_MODEL

_Q0, _F0, _I0, _G0, _P0, _A0, _B0 = (0, KDIM, 2 * KDIM, 2 * KDIM + WIDTH, 2 * KDIM + 2 * WIDTH,
                                     2 * KDIM + 2 * WIDTH + POOL_WIDTH,
                                     2 * KDIM + 2 * WIDTH + POOL_WIDTH + D_MODEL)

LEVELS = (32, 16, 8, 4, 2, 1)
MXU_LEVELS = tuple(c for c in LEVELS if c < SUBLANES)
VMEM_LIMIT_BYTES = 60 * 1024 * 1024
MIXER_TILE = 512
FFN_TILE = 1024

_GELU_K1 = -2.0 * math.sqrt(2.0 / math.pi) * math.log2(math.e)
_GELU_K3 = _GELU_K1 * 0.044715

F32 = jnp.float32
BF16 = jnp.bfloat16


def _decay_sum_matrix():
    c64 = CHUNK
    w = np.zeros(((1 + len(MXU_LEVELS)) * c64, c64), np.float32)
    for t in range(c64):
        w[t, :t + 1] = 1.0
    for li, c in enumerate(MXU_LEVELS):
        base = (1 + li) * c64
        for r in range(c64):
            mid = r - r % (2 * c) + c - 1
            if r > mid:
                w[base + r, mid + 1:r + 1] = 1.0
            else:
                w[base + r, r + 1:mid + 1] = 1.0
    return np.concatenate([w, w], axis=1)


def _rms(x, gain):
    ms = jnp.mean(x * x, axis=-1, keepdims=True)
    return x * lax.rsqrt(ms + EPS) * gain


def _sigmoid(x):
    return 0.5 * jnp.tanh(0.5 * x) + 0.5


def _dot(a, b):
    return jnp.dot(a, b, preferred_element_type=F32)


def _lower_bound(lbraw, layer):
    rows = [lbraw[j:j + 1, :] for j in range(DEPTH)]
    m = functools.reduce(jnp.maximum, rows)
    es = [jnp.exp(r - m) for r in rows]
    tot = functools.reduce(lambda a, b: a + b, es)
    gam = [e / tot for e in es]
    cum = functools.reduce(lambda a, b: a + b, gam[:layer + 1])
    return jnp.clip(cum - gam[0], 0.0, 1.0)


def _mixer_kernel(h_ref, st0_ref, ph0_ref, gpre_ref, win_ref, lbraw_ref, ogain_ref, wbh_ref, pproj_ref,
                  pscale_ref, wbp_ref, wout_ref, gpost_ref, wexp_ref, *rest, tile, layer, meta):
    if meta:
        o_ref, st_out_ref, ph_out_ref, *scratch = rest
    else:
        o_ref, *scratch = rest
    proj_scr, late_scr, u_scr, state_scr, pbuf_scr, a_scr, o_scr, qb_scr, kk_scr, dec_scr, sprev_scr = scratch
    T = tile
    C = CHUNK

    @pl.when(pl.program_id(1) == 0)
    def _():
        state_scr[...] = st0_ref[...]
        pbuf_scr[0:POOL_HIST, :] = ph0_ref[...]

    h = h_ref[...]
    u_scr[...] = _rms(h, gpre_ref[...]).astype(BF16)
    nch = T // C
    proj_scr[...] = _dot(u_scr[...], win_ref[:, 0:_P0])
    late_blocks = (IN_COLS - _P0) // PROJ_BLOCK
    per_step = -(-late_blocks // (nch + 2))

    def project_late(i):
        first = (i + 1) * per_step
        for b in range(max(0, min(per_step, late_blocks - first))):
            start = (first + b) * PROJ_BLOCK
            late_scr[:, start:start + PROJ_BLOCK] = _dot(
                u_scr[...], win_ref[:, _P0 + start:_P0 + start + PROJ_BLOCK])

    lb = _lower_bound(lbraw_ref[...], layer)
    oml = 1.0 - lb
    ogain = ogain_ref[...]
    wexp = wexp_ref[...]
    ti = lax.broadcasted_iota(jnp.int32, (C, C), 0)
    si = lax.broadcasted_iota(jnp.int32, (C, C), 1)
    xor = ti ^ si
    level_masks = [(xor >= c) & (xor < 2 * c) for c in LEVELS]
    causal = ti >= si

    def chunk_rows(ci):
        return slice(ci * C, (ci + 1) * C)

    heads = [slice(hd * HEAD_K, (hd + 1) * HEAD_K) for hd in range(HEADS)]

    def step(i, par, finish=True, scores=True, gates=True):
        prv = 1 - par
        if finish:
            o_inter = [_dot(qb_scr[prv, :, sl], sprev_scr[prv, hd]) for hd, sl in enumerate(heads)]
        project_late(i)
        if scores:
            rows = chunk_rows(i)
            dec = dec_scr.at[par]
            q = proj_scr[rows, _Q0:_Q0 + KDIM]
            v = proj_scr[rows, _I0:_I0 + WIDTH]
            kk = kk_scr[par]
            for hd, sl in enumerate(heads):
                kdec = (kk[:, sl] * dec[C:2 * C, sl]).astype(BF16)
                st_t = state_scr[hd]
                sprev_scr[par, hd] = st_t.T.astype(BF16)
                state_scr[hd] = st_t * dec[C - 1:C, sl] + _dot(v[:, sl].T.astype(BF16), kdec)
            qb_scr[par] = (q * dec[0:C, :]).astype(BF16)
            scs = []
            for hd, sl in enumerate(heads):
                qh, kh = q[:, sl], kk[:, sl]
                sc = _dot(qh.astype(BF16), kh.T.astype(BF16))
                for li in range(len(LEVELS)):
                    dl = dec[(2 + li) * C:(3 + li) * C, sl]
                    s_l = _dot((qh * dl).astype(BF16), (kh * dl).T.astype(BF16))
                    sc = jnp.where(level_masks[li], s_l, sc)
                scs.append(jnp.where(causal, sc, 0.0).astype(BF16))
        if gates:
            fl = proj_scr[chunk_rows(i + 1), _F0:_F0 + KDIM]
            e = jnp.exp(-jnp.abs(fl))
            r = 1.0 / (1.0 + e)
            er = e * r
            nonneg = fl >= 0.0
            sig = jnp.where(nonneg, r, er)
            sig_neg = jnp.where(nonneg, er, r)
            f = lb + oml * sig
            lf = jnp.log2(jnp.maximum(f, LOG_FLOOR))
            kk_scr[prv] = oml * sig_neg
            lf_hi = lf.astype(BF16)
            lf_lo = (lf - lf_hi.astype(F32)).astype(BF16)
            expo = _dot(wexp, jnp.concatenate([lf_hi, lf_lo], axis=0))
            b = expo[0:C, :]
            dec_scr[prv, 0:C, :] = jnp.exp2(b)
            dec_scr[prv, C:2 * C, :] = jnp.exp2(b[C - 1:C, :] - b)
            row = 2
            for c in LEVELS:
                if c in MXU_LEVELS:
                    k = 1 + MXU_LEVELS.index(c)
                    blk = expo[k * C:(k + 1) * C, :]
                else:
                    parts = []
                    for r0 in range(0, C, 2 * c):
                        mid = b[r0 + c - 1:r0 + c, :]
                        parts += [mid - b[r0:r0 + c, :], b[r0 + c:r0 + 2 * c, :] - mid]
                    blk = jnp.concatenate(parts, axis=0)
                dec_scr[prv, row * C:(row + 1) * C, :] = jnp.exp2(blk)
                row += 1
        if finish:
            rows_f = chunk_rows(i - 1)
            g = proj_scr[rows_f, _G0:_G0 + WIDTH]
            for hd, sl in enumerate(heads):
                o = o_scr[prv, :, sl] + o_inter[hd]
                o = o * lax.rsqrt(jnp.mean(o * o, axis=-1, keepdims=True) + EPS)
                a_scr[rows_f, sl] = o * ogain[:, sl] * _sigmoid(g[:, sl])
        if scores:
            for hd, sl in enumerate(heads):
                o_scr[par, :, sl] = _dot(scs[hd], v[:, sl].astype(BF16))

    step(-1, 1, finish=False, scores=False)
    for i in range(nch):
        step(i, i % 2, finish=i > 0, gates=i < nch - 1)
    step(nch, nch % 2, scores=False, gates=False)

    vp = late_scr[:, 0:POOL_WIDTH]
    pbuf_scr[POOL_HIST:POOL_HIST + T, :] = vp
    ys = []
    for gi, w in enumerate(POOL_WINDOWS):
        cs = slice(gi * POOL_GROUP, (gi + 1) * POOL_GROUP)
        s = pbuf_scr[:, cs]
        span = 1
        while span < w:
            s = s + pltpu.roll(s, span, 0)
            span *= 2
        s = s[POOL_HIST:]
        if meta:
            pos = lax.broadcasted_iota(jnp.int32, (T, 1), 0) - (T - N_META)
            cnt = jnp.clip(pos + 1, 1, w).astype(F32)
            pooled = s / cnt - vp[:, cs]
        else:
            pooled = s * (1.0 / w) - vp[:, cs]
        ys.append(_dot(pooled.astype(BF16), pproj_ref[gi]))
    p = jnp.concatenate(ys, axis=-1) * pscale_ref[...]
    hist = pbuf_scr[T:T + POOL_HIST, :]
    pbuf_scr[0:POOL_HIST, :] = hist

    br_a = _dot(a_scr[...].astype(BF16), wbh_ref[...])
    br_p = _dot(p.astype(BF16), wbp_ref[...])
    z = (_sigmoid(late_scr[:, _A0 - _P0:_B0 - _P0]) * br_a
         + _sigmoid(late_scr[:, _B0 - _P0:IN_COLS - _P0]) * br_p)
    zz = _dot(z.astype(BF16), wout_ref[...])
    o_ref[...] = h + _rms(zz, gpost_ref[...])

    if meta:
        st_out_ref[...] = state_scr[...]
        ph_out_ref[...] = hist


def _ffn_kernel(h_ref, c0_ref, gpre_ref, wg_ref, wu_ref, cw_ref, cb_ref, wd_ref, gpost_ref, *rest,
                tile, meta):
    if meta:
        o_ref, c_out_ref, hist_scr, act_scr = rest
    else:
        o_ref, hist_scr, act_scr = rest
    T = tile

    @pl.when(pl.program_id(1) == 0)
    def _():
        hist_scr[...] = c0_ref[...]

    h = h_ref[...]
    u = _rms(h, gpre_ref[...]).astype(BF16)
    row8 = lax.broadcasted_iota(jnp.int32, (SUBLANES, FF_BLOCK), 0)

    def shifted(g, prev, k):
        rolled = pltpu.roll(g, k, 0)
        top = jnp.where(row8 < k, pltpu.roll(prev, k, 0), rolled[0:SUBLANES])
        return jnp.concatenate([top, rolled[SUBLANES:]], axis=0)

    for j in range(D_FF // FF_BLOCK):
        cols = slice(j * FF_BLOCK, (j + 1) * FF_BLOCK)
        g = _dot(u, wg_ref[:, cols])
        up = _dot(u, wu_ref[:, cols])
        prev = hist_scr[:, cols]
        hist_scr[:, cols] = g[T - SUBLANES:T]
        cw = cw_ref[:, cols]
        gc = cb_ref[:, cols] + shifted(g, prev, 2) * cw[0:1] + shifted(g, prev, 1) * cw[1:2] + g * cw[2:3]
        e = jnp.exp2(gc * (_GELU_K1 + _GELU_K3 * (gc * gc)))
        act_scr[:, cols] = ((gc * up) * (1.0 / (1.0 + e))).astype(BF16)

    y = _dot(act_scr[...], wd_ref[...])
    o_ref[...] = h + _rms(y, gpost_ref[...])
    if meta:
        c_out_ref[...] = hist_scr[...]


class _Layer:
    def __init__(self, stacked, layer):
        self.array, self.layer = stacked, layer


def _const_spec(operand):
    if isinstance(operand, _Layer):
        shape, layer = operand.array.shape[1:], operand.layer
        return pl.BlockSpec((None,) + shape, lambda b, t: (layer,) + (0,) * len(shape),
                            pipeline_mode=pl.Buffered(1))
    nd = len(operand.shape)
    return pl.BlockSpec(operand.shape, lambda b, t: (0,) * nd, pipeline_mode=pl.Buffered(1))


def _arrays(operands):
    return [o.array if isinstance(o, _Layer) else o for o in operands]


def _mixer(h, st0, ph0, weights, wexp, *, tile, layer, meta):
    bsz, seq, _ = h.shape
    nt = seq // tile
    tok_spec = pl.BlockSpec((None, tile, D_MODEL), lambda b, t: (b, t, 0))
    consts = (st0, ph0) + tuple(weights) + (wexp,)
    out_shape = [jax.ShapeDtypeStruct(h.shape, F32)]
    out_specs = [tok_spec]
    if meta:
        out_shape += [jax.ShapeDtypeStruct(st0.shape, F32), jax.ShapeDtypeStruct(ph0.shape, F32)]
        out_specs += [_const_spec(st0), _const_spec(ph0)]
    out = pl.pallas_call(
        functools.partial(_mixer_kernel, tile=tile, layer=layer, meta=meta),
        grid=(bsz, nt),
        in_specs=[tok_spec] + [_const_spec(c) for c in consts],
        out_specs=out_specs,
        out_shape=out_shape,
        scratch_shapes=[
            pltpu.VMEM((tile, _P0), F32),
            pltpu.VMEM((tile, IN_COLS - _P0), F32),
            pltpu.VMEM((tile, D_MODEL), BF16),
            pltpu.VMEM((HEADS, HEAD_V, HEAD_K), F32),
            pltpu.VMEM((POOL_HIST + tile, POOL_WIDTH), F32),
            pltpu.VMEM((tile, WIDTH), F32),
            pltpu.VMEM((2, CHUNK, WIDTH), F32),
            pltpu.VMEM((2, CHUNK, KDIM), BF16),
            pltpu.VMEM((2, CHUNK, KDIM), F32),
            pltpu.VMEM((2, (2 + len(LEVELS)) * CHUNK, KDIM), F32),
            pltpu.VMEM((2, HEADS, HEAD_K, HEAD_V), BF16),
        ],
        compiler_params=pltpu.CompilerParams(
            dimension_semantics=("arbitrary", "arbitrary"), vmem_limit_bytes=VMEM_LIMIT_BYTES),
        name="mixer_meta" if meta else "mixer",
    )(h, *_arrays(consts))
    return out if meta else out[0]


def _ffn(h, c0, weights, *, tile, meta):
    bsz, seq, _ = h.shape
    nt = seq // tile
    tok_spec = pl.BlockSpec((None, tile, D_MODEL), lambda b, t: (b, t, 0))
    consts = (c0,) + tuple(weights)
    out_shape = [jax.ShapeDtypeStruct(h.shape, F32)]
    out_specs = [tok_spec]
    if meta:
        out_shape += [jax.ShapeDtypeStruct(c0.shape, F32)]
        out_specs += [_const_spec(c0)]
    out = pl.pallas_call(
        functools.partial(_ffn_kernel, tile=tile, meta=meta),
        grid=(bsz, nt),
        in_specs=[tok_spec] + [_const_spec(c) for c in consts],
        out_specs=out_specs,
        out_shape=out_shape,
        scratch_shapes=[
            pltpu.VMEM((SUBLANES, D_FF), F32),
            pltpu.VMEM((tile, D_FF), BF16),
        ],
        compiler_params=pltpu.CompilerParams(
            dimension_semantics=("arbitrary", "arbitrary"), vmem_limit_bytes=VMEM_LIMIT_BYTES),
        name="ffn_meta" if meta else "ffn",
    )(h, *_arrays(consts))
    return out if meta else out[0]


def _pick_tile(seq, largest):
    t = largest
    while t >= CHUNK:
        if seq % t == 0:
            return t
        t //= 2
    raise ValueError(f"sequence length {seq} must be a multiple of {CHUNK}")


def kernel(x, meta_tokens, mix_norm_pre, mix_norm_post, w_in, hgrn_lower_bounds, hgrn_out_norm, w_branch_hgrn, pool_proj, pool_scale, w_branch_pool, w_out, ffn_norm_pre, ffn_norm_post, ffn_w_gate, ffn_w_up, ffn_conv_w, ffn_conv_b, ffn_w_down):
    bsz, seq, d = x.shape
    assert d == D_MODEL and meta_tokens.shape == (N_META, D_MODEL)
    mix_tile = _pick_tile(seq, MIXER_TILE)
    ffn_tile = _pick_tile(seq, FFN_TILE)
    wexp = jnp.asarray(_decay_sum_matrix(), BF16)
    row = lambda a: a.reshape(1, -1).astype(F32)

    h = x.astype(F32)
    hm = jnp.concatenate([jnp.zeros((CHUNK - N_META, D_MODEL), F32), meta_tokens.astype(F32)], axis=0)[None]
    st_zero = jnp.zeros((HEADS, HEAD_V, HEAD_K), F32)
    ph_zero = jnp.zeros((POOL_HIST, POOL_WIDTH), F32)
    cv_zero = jnp.zeros((SUBLANES, D_FF), F32)
    lbraw = hgrn_lower_bounds.astype(F32)

    bf = lambda a: a.astype(BF16)
    w_in_b, w_bh_b, pproj_b, w_bp_b, w_out_b = bf(w_in), bf(w_branch_hgrn), bf(pool_proj), bf(w_branch_pool), bf(w_out)
    w_gate_b, w_up_b, w_down_b = bf(ffn_w_gate), bf(ffn_w_up), bf(ffn_w_down)
    conv_w = ffn_conv_w.astype(F32)

    for l in range(DEPTH):
        mix_w = (row(mix_norm_pre[l]), _Layer(w_in_b, l), lbraw, row(hgrn_out_norm[l]),
                 _Layer(w_bh_b, l), _Layer(pproj_b, l), row(pool_scale[l]),
                 _Layer(w_bp_b, l), _Layer(w_out_b, l), row(mix_norm_post[l]))
        ffn_w = (row(ffn_norm_pre[l]), _Layer(w_gate_b, l), _Layer(w_up_b, l),
                 _Layer(conv_w, l), row(ffn_conv_b[l]), _Layer(w_down_b, l),
                 row(ffn_norm_post[l]))
        hm, st, ph = _mixer(hm, st_zero, ph_zero, mix_w, wexp, tile=CHUNK, layer=l, meta=True)
        h = _mixer(h, st, ph, mix_w, wexp, tile=mix_tile, layer=l, meta=False)
        hm, cv = _ffn(hm, cv_zero, ffn_w, tile=CHUNK, meta=True)
        h = _ffn(h, cv, ffn_w, tile=ffn_tile, meta=False)
    return h.astype(x.dtype)
```

```python
import functools
import math

import numpy as np
import jax
import jax.numpy as jnp
from jax import lax
from jax.experimental import pallas as pl
from jax.experimental.pallas import tpu as pltpu

D_MODEL = 1024
DEPTH = 2
N_META = 16
HEADS = 4
HEAD_K = 128
HEAD_V = 128
KDIM = HEADS * HEAD_K
WIDTH = HEADS * HEAD_V
CHUNK = 128
POOL_WINDOWS = (2, 4, 8, 16)
POOL_GROUP = 128
POOL_WIDTH = POOL_GROUP * len(POOL_WINDOWS)
POOL_HIST = 16
D_FF = 2816
FF_BLOCK = 256
PROJ_BLOCK = 256
CONV_WIDTH = 3
SUBLANES = 8
EPS = 1e-6
LOG_FLOOR = 1e-30
IN_COLS = 2 * KDIM + 2 * WIDTH + POOL_WIDTH + 2 * D_MODEL

_Q0, _F0, _I0, _G0, _P0, _A0, _B0 = (0, KDIM, 2 * KDIM, 2 * KDIM + WIDTH, 2 * KDIM + 2 * WIDTH,
                                     2 * KDIM + 2 * WIDTH + POOL_WIDTH,
                                     2 * KDIM + 2 * WIDTH + POOL_WIDTH + D_MODEL)

LEVELS = (64, 32, 16, 8, 4, 2, 1)
MXU_LEVELS = tuple(c for c in LEVELS if c < SUBLANES)
VMEM_LIMIT_BYTES = 60 * 1024 * 1024
MIXER_TILE = 512
FFN_TILE = 1024

_GELU_K1 = -2.0 * math.sqrt(2.0 / math.pi) * math.log2(math.e)
_GELU_K3 = _GELU_K1 * 0.044715

F32 = jnp.float32
BF16 = jnp.bfloat16


def _decay_sum_matrix():
    c64 = CHUNK
    w = np.zeros(((1 + len(MXU_LEVELS)) * c64, c64), np.float32)
    for t in range(c64):
        w[t, :t + 1] = 1.0
    for li, c in enumerate(MXU_LEVELS):
        base = (1 + li) * c64
        for r in range(c64):
            mid = r - r % (2 * c) + c - 1
            if r > mid:
                w[base + r, mid + 1:r + 1] = 1.0
            else:
                w[base + r, r + 1:mid + 1] = 1.0
    return np.concatenate([w, w], axis=1)


def _rms(x, gain):
    ms = jnp.mean(x * x, axis=-1, keepdims=True)
    return x * lax.rsqrt(ms + EPS) * gain


def _sigmoid(x):
    return 1.0 / (1.0 + jnp.exp(-x))


def _dot(a, b):
    return jnp.dot(a, b, preferred_element_type=F32)


def _lower_bound(lbraw, layer):
    rows = [lbraw[j:j + 1, :] for j in range(DEPTH)]
    m = functools.reduce(jnp.maximum, rows)
    es = [jnp.exp(r - m) for r in rows]
    tot = functools.reduce(lambda a, b: a + b, es)
    gam = [e / tot for e in es]
    cum = functools.reduce(lambda a, b: a + b, gam[:layer + 1])
    return jnp.clip(cum - gam[0], 0.0, 1.0)


def _mixer_kernel(h_ref, st0_ref, ph0_ref, gpre_ref, win_ref, lbraw_ref, ogain_ref, wbh_ref, pproj_ref,
                  pscale_ref, wbp_ref, wout_ref, gpost_ref, wexp_ref, *rest, tile, layer, meta):
    if meta:
        o_ref, st_out_ref, ph_out_ref, *scratch = rest
    else:
        o_ref, *scratch = rest
    proj_scr, late_scr, u_scr, state_scr, pbuf_scr, a_scr, o_scr, qb_scr, kk_scr, dec_scr, sprev_scr = scratch
    T = tile
    C = CHUNK

    @pl.when(pl.program_id(1) == 0)
    def _():
        state_scr[...] = st0_ref[...]
        pbuf_scr[0:POOL_HIST, :] = ph0_ref[...]

    h = h_ref[...]
    u_scr[...] = _rms(h, gpre_ref[...]).astype(BF16)
    nch = T // C
    proj_scr[...] = _dot(u_scr[...], win_ref[:, 0:_P0])
    late_blocks = (IN_COLS - _P0) // PROJ_BLOCK
    per_step = -(-late_blocks // (nch + 2))

    def project_late(i):
        first = (i + 1) * per_step
        for b in range(max(0, min(per_step, late_blocks - first))):
            start = (first + b) * PROJ_BLOCK
            late_scr[:, start:start + PROJ_BLOCK] = _dot(
                u_scr[...], win_ref[:, _P0 + start:_P0 + start + PROJ_BLOCK])

    lb = _lower_bound(lbraw_ref[...], layer)
    oml = 1.0 - lb
    ogain = ogain_ref[...]
    wexp = wexp_ref[...]
    ti = lax.broadcasted_iota(jnp.int32, (C, C), 0)
    si = lax.broadcasted_iota(jnp.int32, (C, C), 1)
    xor = ti ^ si
    level_masks = [(xor >= c) & (xor < 2 * c) for c in LEVELS]
    causal = ti >= si

    def chunk_rows(ci):
        return slice(ci * C, (ci + 1) * C)

    heads = [slice(hd * HEAD_K, (hd + 1) * HEAD_K) for hd in range(HEADS)]

    def step(i, par, finish=True, scores=True, gates=True):
        prv = 1 - par
        if finish:
            o_inter = [_dot(qb_scr[prv, :, sl], sprev_scr[prv, hd]) for hd, sl in enumerate(heads)]
        project_late(i)
        if scores:
            rows = chunk_rows(i)
            dec = dec_scr.at[par]
            q = proj_scr[rows, _Q0:_Q0 + KDIM]
            v = proj_scr[rows, _I0:_I0 + WIDTH]
            kk = kk_scr[par]
            for hd, sl in enumerate(heads):
                kdec = (kk[:, sl] * dec[C:2 * C, sl]).astype(BF16)
                st_t = state_scr[hd]
                sprev_scr[par, hd] = st_t.T.astype(BF16)
                state_scr[hd] = st_t * dec[C - 1:C, sl] + _dot(v[:, sl].T.astype(BF16), kdec)
            qb_scr[par] = (q * dec[0:C, :]).astype(BF16)
            scs = []
            for hd, sl in enumerate(heads):
                qh, kh = q[:, sl], kk[:, sl]
                sc = _dot(qh.astype(BF16), kh.T.astype(BF16))
                for li in range(len(LEVELS)):
                    dl = dec[(2 + li) * C:(3 + li) * C, sl]
                    s_l = _dot((qh * dl).astype(BF16), (kh * dl).T.astype(BF16))
                    sc = jnp.where(level_masks[li], s_l, sc)
                scs.append(jnp.where(causal, sc, 0.0).astype(BF16))
        if gates:
            fl = proj_scr[chunk_rows(i + 1), _F0:_F0 + KDIM]
            e = jnp.exp(-jnp.abs(fl))
            r = 1.0 / (1.0 + e)
            er = e * r
            nonneg = fl >= 0.0
            sig = jnp.where(nonneg, r, er)
            sig_neg = jnp.where(nonneg, er, r)
            f = lb + oml * sig
            lf = jnp.log2(jnp.maximum(f, LOG_FLOOR))
            kk_scr[prv] = oml * sig_neg
            lf_hi = lf.astype(BF16)
            lf_lo = (lf - lf_hi.astype(F32)).astype(BF16)
            expo = _dot(wexp, jnp.concatenate([lf_hi, lf_lo], axis=0))
            b = expo[0:C, :]
            dec_scr[prv, 0:C, :] = jnp.exp2(b)
            dec_scr[prv, C:2 * C, :] = jnp.exp2(b[C - 1:C, :] - b)
            row = 2
            for c in LEVELS:
                if c in MXU_LEVELS:
                    k = 1 + MXU_LEVELS.index(c)
                    blk = expo[k * C:(k + 1) * C, :]
                else:
                    parts = []
                    for r0 in range(0, C, 2 * c):
                        mid = b[r0 + c - 1:r0 + c, :]
                        parts += [mid - b[r0:r0 + c, :], b[r0 + c:r0 + 2 * c, :] - mid]
                    blk = jnp.concatenate(parts, axis=0)
                dec_scr[prv, row * C:(row + 1) * C, :] = jnp.exp2(blk)
                row += 1
        if finish:
            rows_f = chunk_rows(i - 1)
            g = proj_scr[rows_f, _G0:_G0 + WIDTH]
            for hd, sl in enumerate(heads):
                o = o_scr[prv, :, sl] + o_inter[hd]
                o = o * lax.rsqrt(jnp.mean(o * o, axis=-1, keepdims=True) + EPS)
                a_scr[rows_f, sl] = o * ogain[:, sl] * _sigmoid(g[:, sl])
        if scores:
            for hd, sl in enumerate(heads):
                o_scr[par, :, sl] = _dot(scs[hd], v[:, sl].astype(BF16))

    step(-1, 1, finish=False, scores=False)
    for i in range(nch):
        step(i, i % 2, finish=i > 0, gates=i < nch - 1)
    step(nch, nch % 2, scores=False, gates=False)

    vp = late_scr[:, 0:POOL_WIDTH]
    pbuf_scr[POOL_HIST:POOL_HIST + T, :] = vp
    ys = []
    for gi, w in enumerate(POOL_WINDOWS):
        cs = slice(gi * POOL_GROUP, (gi + 1) * POOL_GROUP)
        s = pbuf_scr[:, cs]
        span = 1
        while span < w:
            s = s + pltpu.roll(s, span, 0)
            span *= 2
        s = s[POOL_HIST:]
        if meta:
            pos = lax.broadcasted_iota(jnp.int32, (T, 1), 0) - (T - N_META)
            cnt = jnp.clip(pos + 1, 1, w).astype(F32)
            pooled = s / cnt - vp[:, cs]
        else:
            pooled = s * (1.0 / w) - vp[:, cs]
        ys.append(_dot(pooled.astype(BF16), pproj_ref[gi]))
    p = jnp.concatenate(ys, axis=-1) * pscale_ref[...]
    hist = pbuf_scr[T:T + POOL_HIST, :]
    pbuf_scr[0:POOL_HIST, :] = hist

    br_a = _dot(a_scr[...].astype(BF16), wbh_ref[...])
    br_p = _dot(p.astype(BF16), wbp_ref[...])
    z = (_sigmoid(late_scr[:, _A0 - _P0:_B0 - _P0]) * br_a
         + _sigmoid(late_scr[:, _B0 - _P0:IN_COLS - _P0]) * br_p)
    zz = _dot(z.astype(BF16), wout_ref[...])
    o_ref[...] = h + _rms(zz, gpost_ref[...])

    if meta:
        st_out_ref[...] = state_scr[...]
        ph_out_ref[...] = hist


def _ffn_kernel(h_ref, c0_ref, gpre_ref, wg_ref, wu_ref, cw_ref, cb_ref, wd_ref, gpost_ref, *rest,
                tile, meta):
    if meta:
        o_ref, c_out_ref, hist_scr, act_scr = rest
    else:
        o_ref, hist_scr, act_scr = rest
    T = tile

    @pl.when(pl.program_id(1) == 0)
    def _():
        hist_scr[...] = c0_ref[...]

    h = h_ref[...]
    u = _rms(h, gpre_ref[...]).astype(BF16)
    row8 = lax.broadcasted_iota(jnp.int32, (SUBLANES, FF_BLOCK), 0)

    def shifted(g, prev, k):
        rolled = pltpu.roll(g, k, 0)
        top = jnp.where(row8 < k, pltpu.roll(prev, k, 0), rolled[0:SUBLANES])
        return jnp.concatenate([top, rolled[SUBLANES:]], axis=0)

    for j in range(D_FF // FF_BLOCK):
        cols = slice(j * FF_BLOCK, (j + 1) * FF_BLOCK)
        g = _dot(u, wg_ref[:, cols])
        up = _dot(u, wu_ref[:, cols])
        prev = hist_scr[:, cols]
        hist_scr[:, cols] = g[T - SUBLANES:T]
        cw = cw_ref[:, cols]
        gc = cb_ref[:, cols] + shifted(g, prev, 2) * cw[0:1] + shifted(g, prev, 1) * cw[1:2] + g * cw[2:3]
        e = jnp.exp2(gc * (_GELU_K1 + _GELU_K3 * (gc * gc)))
        act_scr[:, cols] = ((gc * up) * (1.0 / (1.0 + e))).astype(BF16)

    y = _dot(act_scr[...], wd_ref[...])
    o_ref[...] = h + _rms(y, gpost_ref[...])
    if meta:
        c_out_ref[...] = hist_scr[...]


class _Layer:
    def __init__(self, stacked, layer):
        self.array, self.layer = stacked, layer


def _const_spec(operand):
    if isinstance(operand, _Layer):
        shape, layer = operand.array.shape[1:], operand.layer
        return pl.BlockSpec((None,) + shape, lambda b, t: (layer,) + (0,) * len(shape),
                            pipeline_mode=pl.Buffered(1))
    nd = len(operand.shape)
    return pl.BlockSpec(operand.shape, lambda b, t: (0,) * nd, pipeline_mode=pl.Buffered(1))


def _arrays(operands):
    return [o.array if isinstance(o, _Layer) else o for o in operands]


def _mixer(h, st0, ph0, weights, wexp, *, tile, layer, meta):
    bsz, seq, _ = h.shape
    nt = seq // tile
    tok_spec = pl.BlockSpec((None, tile, D_MODEL), lambda b, t: (b, t, 0))
    consts = (st0, ph0) + tuple(weights) + (wexp,)
    out_shape = [jax.ShapeDtypeStruct(h.shape, F32)]
    out_specs = [tok_spec]
    if meta:
        out_shape += [jax.ShapeDtypeStruct(st0.shape, F32), jax.ShapeDtypeStruct(ph0.shape, F32)]
        out_specs += [_const_spec(st0), _const_spec(ph0)]
    out = pl.pallas_call(
        functools.partial(_mixer_kernel, tile=tile, layer=layer, meta=meta),
        grid=(bsz, nt),
        in_specs=[tok_spec] + [_const_spec(c) for c in consts],
        out_specs=out_specs,
        out_shape=out_shape,
        scratch_shapes=[
            pltpu.VMEM((tile, _P0), F32),
            pltpu.VMEM((tile, IN_COLS - _P0), F32),
            pltpu.VMEM((tile, D_MODEL), BF16),
            pltpu.VMEM((HEADS, HEAD_V, HEAD_K), F32),
            pltpu.VMEM((POOL_HIST + tile, POOL_WIDTH), F32),
            pltpu.VMEM((tile, WIDTH), F32),
            pltpu.VMEM((2, CHUNK, WIDTH), F32),
            pltpu.VMEM((2, CHUNK, KDIM), BF16),
            pltpu.VMEM((2, CHUNK, KDIM), F32),
            pltpu.VMEM((2, (2 + len(LEVELS)) * CHUNK, KDIM), F32),
            pltpu.VMEM((2, HEADS, HEAD_K, HEAD_V), BF16),
        ],
        compiler_params=pltpu.CompilerParams(
            dimension_semantics=("arbitrary", "arbitrary"), vmem_limit_bytes=VMEM_LIMIT_BYTES),
        name="mixer_meta" if meta else "mixer",
    )(h, *_arrays(consts))
    return out if meta else out[0]


def _ffn(h, c0, weights, *, tile, meta):
    bsz, seq, _ = h.shape
    nt = seq // tile
    tok_spec = pl.BlockSpec((None, tile, D_MODEL), lambda b, t: (b, t, 0))
    consts = (c0,) + tuple(weights)
    out_shape = [jax.ShapeDtypeStruct(h.shape, F32)]
    out_specs = [tok_spec]
    if meta:
        out_shape += [jax.ShapeDtypeStruct(c0.shape, F32)]
        out_specs += [_const_spec(c0)]
    out = pl.pallas_call(
        functools.partial(_ffn_kernel, tile=tile, meta=meta),
        grid=(bsz, nt),
        in_specs=[tok_spec] + [_const_spec(c) for c in consts],
        out_specs=out_specs,
        out_shape=out_shape,
        scratch_shapes=[
            pltpu.VMEM((SUBLANES, D_FF), F32),
            pltpu.VMEM((tile, D_FF), BF16),
        ],
        compiler_params=pltpu.CompilerParams(
            dimension_semantics=("arbitrary", "arbitrary"), vmem_limit_bytes=VMEM_LIMIT_BYTES),
        name="ffn_meta" if meta else "ffn",
    )(h, *_arrays(consts))
    return out if meta else out[0]


def _pick_tile(seq, largest):
    t = largest
    while t >= CHUNK:
        if seq % t == 0:
            return t
        t //= 2
    raise ValueError(f"sequence length {seq} must be a multiple of {CHUNK}")


def kernel(x, meta_tokens, mix_norm_pre, mix_norm_post, w_in, hgrn_lower_bounds, hgrn_out_norm, w_branch_hgrn, pool_proj, pool_scale, w_branch_pool, w_out, ffn_norm_pre, ffn_norm_post, ffn_w_gate, ffn_w_up, ffn_conv_w, ffn_conv_b, ffn_w_down):
    bsz, seq, d = x.shape
    assert d == D_MODEL and meta_tokens.shape == (N_META, D_MODEL)
    mix_tile = _pick_tile(seq, MIXER_TILE)
    ffn_tile = _pick_tile(seq, FFN_TILE)
    wexp = jnp.asarray(_decay_sum_matrix(), BF16)
    row = lambda a: a.reshape(1, -1).astype(F32)

    h = x.astype(F32)
    hm = jnp.concatenate([jnp.zeros((CHUNK - N_META, D_MODEL), F32), meta_tokens.astype(F32)], axis=0)[None]
    st_zero = jnp.zeros((HEADS, HEAD_V, HEAD_K), F32)
    ph_zero = jnp.zeros((POOL_HIST, POOL_WIDTH), F32)
    cv_zero = jnp.zeros((SUBLANES, D_FF), F32)
    lbraw = hgrn_lower_bounds.astype(F32)

    bf = lambda a: a.astype(BF16)
    w_in_b, w_bh_b, pproj_b, w_bp_b, w_out_b = bf(w_in), bf(w_branch_hgrn), bf(pool_proj), bf(w_branch_pool), bf(w_out)
    w_gate_b, w_up_b, w_down_b = bf(ffn_w_gate), bf(ffn_w_up), bf(ffn_w_down)
    conv_w = ffn_conv_w.astype(F32)

    for l in range(DEPTH):
        mix_w = (row(mix_norm_pre[l]), _Layer(w_in_b, l), lbraw, row(hgrn_out_norm[l]),
                 _Layer(w_bh_b, l), _Layer(pproj_b, l), row(pool_scale[l]),
                 _Layer(w_bp_b, l), _Layer(w_out_b, l), row(mix_norm_post[l]))
        ffn_w = (row(ffn_norm_pre[l]), _Layer(w_gate_b, l), _Layer(w_up_b, l),
                 _Layer(conv_w, l), row(ffn_conv_b[l]), _Layer(w_down_b, l),
                 row(ffn_norm_post[l]))
        hm, st, ph = _mixer(hm, st_zero, ph_zero, mix_w, wexp, tile=CHUNK, layer=l, meta=True)
        h = _mixer(h, st, ph, mix_w, wexp, tile=mix_tile, layer=l, meta=False)
        hm, cv = _ffn(hm, cv_zero, ffn_w, tile=CHUNK, meta=True)
        h = _ffn(h, cv, ffn_w, tile=ffn_tile, meta=False)
    return h.astype(x.dtype)
```
